```python
import jax, jax.numpy as jnp
from jax import lax
import numpy as np


D_MODEL = 1024
BATCH = 4
SEQ = 8192
DEPTH = 2
DEC_BATCH = 1
DEC_SEQ = 16384
PAST_LEN = 128

GRID_W = 64
HEAD_DIM = 64
NA_HEADS = 8
NA_KH_MAX = 8
NA_KW = 16
MLA_HEADS = 8
MLA_Q_RANK = 384
MLA_KV_RANK = 256
MLA_NOPE = 64
MLA_ROPE = 32
MLA_V = 64
MLA_THETA = 10000.0
GQA_Q_HEADS = 16
GQA_KV_HEADS = 4
WINDOW = 128
ROPE_THETA = 500000.0
ROT_DIM = HEAD_DIM // 4
D_FF = 2816
CONV_W = 3
BLOCK_Q = 128
ALPHA = (2 * DEPTH) ** 0.25
BETA = (8 * DEPTH) ** -0.25
NORM_EPS = 1e-5
NA_WIDTH = NA_HEADS * HEAD_DIM
L0_IN = 3 * NA_WIDTH + MLA_Q_RANK + MLA_KV_RANK + MLA_ROPE
L1_IN = (GQA_Q_HEADS + 2 * GQA_KV_HEADS) * HEAD_DIM

kernel_name = "hybrid_na_mla_swa_encoder"


def layer_norm(x, g, b):
    xf = x.astype(jnp.float32)
    mu = jnp.mean(xf, axis=-1, keepdims=True)
    xc = xf - mu
    var = jnp.mean(xc * xc, axis=-1, keepdims=True)
    y = xc * lax.rsqrt(var + NORM_EPS)
    return (y * g.astype(jnp.float32) + b.astype(jnp.float32)).astype(x.dtype)


def rms_norm(x, g):
    xf = x.astype(jnp.float32)
    y = xf * lax.rsqrt(jnp.mean(xf * xf, axis=-1, keepdims=True) + NORM_EPS)
    return (y * g.astype(jnp.float32)).astype(x.dtype)


def rope(x, theta, rot_dim):
    S = x.shape[1]
    half = rot_dim // 2
    inv = 1.0 / (theta ** (jnp.arange(0, rot_dim, 2, dtype=jnp.float32) / rot_dim))
    ang = jnp.arange(S, dtype=jnp.float32)[:, None] * inv[None, :]
    cos = jnp.cos(ang)[None, :, None, :]
    sin = jnp.sin(ang)[None, :, None, :]
    xr = x[..., :rot_dim].astype(jnp.float32)
    x1, x2 = xr[..., :half], xr[..., half:]
    rot = jnp.concatenate([x1 * cos - x2 * sin, x2 * cos + x1 * sin], axis=-1)
    return jnp.concatenate([rot.astype(x.dtype), x[..., rot_dim:]], axis=-1)


def neighbourhood_attention(q, k, v, rpb):
    B, S, H, dh = q.shape
    rows = S // GRID_W
    kh = min(NA_KH_MAX, rows)
    n = kh * NA_KW
    col = jnp.arange(GRID_W)
    cs = jnp.clip(col - NA_KW // 2, 0, GRID_W - NA_KW)
    key_cols = cs[:, None] + jnp.arange(NA_KW)[None, :]
    dc = jnp.broadcast_to((key_cols - col[:, None])[:, None, :], (GRID_W, kh, NA_KW))
    scale = dh ** -0.5

    def one_row(args):
        q_r, r = args
        rs = jnp.clip(r - kh // 2, 0, rows - kh)
        key_rows = rs + jnp.arange(kh)
        idx = (key_rows[None, :, None] * GRID_W + key_cols[:, None, :]).reshape(GRID_W, n)
        dr = jnp.broadcast_to((key_rows - r)[None, :, None], (GRID_W, kh, NA_KW))
        bias = rpb[:, dr + NA_KH_MAX - 1, dc + NA_KW - 1].reshape(H, GRID_W, n)
        k_g = k[:, idx]
        v_g = v[:, idx]
        s = jnp.einsum('bqhd,bqnhd->bhqn', q_r, k_g).astype(jnp.float32) * scale
        s = s + bias.astype(jnp.float32)[None]
        p = jax.nn.softmax(s, axis=-1).astype(v.dtype)
        return jnp.einsum('bhqn,bqnhd->bqhd', p, v_g)

    q_rows = q.reshape(B, rows, GRID_W, H, dh).transpose(1, 0, 2, 3, 4)
    out = lax.map(one_row, (q_rows, jnp.arange(rows)))
    return out.transpose(1, 0, 2, 3, 4).reshape(B, S, H * dh)


def mla_attention(q_nope, q_rope, k_nope, k_rope, v):
    B, S, H, _ = q_nope.shape
    nb = S // BLOCK_Q
    scale = (MLA_NOPE + MLA_ROPE) ** -0.5

    def blk(args):
        qn, qr = args
        s = jnp.einsum('bqhd,bkhd->bhqk', qn, k_nope) + jnp.einsum('bqhr,bkr->bhqk', qr, k_rope)
        p = jax.nn.softmax(s.astype(jnp.float32) * scale, axis=-1).astype(v.dtype)
        return jnp.einsum('bhqk,bkhd->bqhd', p, v)

    def to_blocks(t):
        return t.reshape(B, nb, BLOCK_Q, *t.shape[2:]).swapaxes(0, 1)

    out = lax.map(blk, (to_blocks(q_nope), to_blocks(q_rope)))
    return out.swapaxes(0, 1).reshape(B, S, H * MLA_V)


def window_gqa_sink(q, k, v, sinks):
    B, S, Hq, dh = q.shape
    Hkv = k.shape[2]
    G = Hq // Hkv
    nb = S // BLOCK_Q
    pad = ((0, 0), (BLOCK_Q, BLOCK_Q), (0, 0), (0, 0))
    kp = jnp.pad(k, pad).reshape(B, nb + 2, BLOCK_Q, Hkv, dh)
    vp = jnp.pad(v, pad).reshape(B, nb + 2, BLOCK_Q, Hkv, dh)
    kb = jnp.concatenate([kp[:, :-2], kp[:, 1:-1], kp[:, 2:]], axis=2)
    vb = jnp.concatenate([vp[:, :-2], vp[:, 1:-1], vp[:, 2:]], axis=2)
    qb = q.reshape(B, nb, BLOCK_Q, Hkv, G, dh)
    s = jnp.einsum('bnqkgd,bnjkd->bnkgqj', qb, kb).astype(jnp.float32) * (dh ** -0.5)
    blk = jnp.arange(nb)[:, None, None]
    qpos = blk * BLOCK_Q + jnp.arange(BLOCK_Q)[None, :, None]
    kpos = (blk - 1) * BLOCK_Q + jnp.arange(3 * BLOCK_Q)[None, None, :]
    valid = (jnp.abs(qpos - kpos) <= WINDOW) & (kpos >= 0) & (kpos < S)
    s = jnp.where(valid[None, :, None, None], s, -jnp.inf)
    sink = sinks.astype(jnp.float32).reshape(1, 1, Hkv, G, 1, 1)
    m = jnp.maximum(jnp.max(s, axis=-1, keepdims=True), sink)
    p = jnp.exp(s - m)
    p = (p / (jnp.sum(p, axis=-1, keepdims=True) + jnp.exp(sink - m))).astype(v.dtype)
    out = jnp.einsum('bnkgqj,bnjkd->bnqkgd', p, vb)
    return out.reshape(B, S, Hq * dh)


def even_layer(x, w_in, rpb, g_qn, w_q_up, g_kvn, w_kv_up, w_out, ln_g, ln_b):
    B, S, _ = x.shape
    h = x @ w_in
    sizes = [NA_WIDTH, NA_WIDTH, NA_WIDTH, MLA_Q_RANK, MLA_KV_RANK]
    splits = [int(c) for c in np.cumsum(sizes)]
    qa, ka, va, q_lat, kv_lat, k_r = jnp.split(h, splits, axis=-1)
    shp = (B, S, NA_HEADS, HEAD_DIM)
    a_out = neighbourhood_attention(qa.reshape(shp), ka.reshape(shp), va.reshape(shp), rpb)
    qm = (rms_norm(q_lat, g_qn) @ w_q_up).reshape(B, S, MLA_HEADS, MLA_NOPE + MLA_ROPE)
    q_nope = qm[..., :MLA_NOPE]
    q_rope = rope(qm[..., MLA_NOPE:], MLA_THETA, MLA_ROPE)
    kvm = (rms_norm(kv_lat, g_kvn) @ w_kv_up).reshape(B, S, MLA_HEADS, MLA_NOPE + MLA_V)
    k_nope, v_m = kvm[..., :MLA_NOPE], kvm[..., MLA_NOPE:]
    k_rope = rope(k_r[:, :, None, :], MLA_THETA, MLA_ROPE)[:, :, 0]
    b_out = mla_attention(q_nope, q_rope, k_nope, k_rope, v_m)
    mix = jnp.concatenate([a_out, b_out], axis=-1) @ w_out
    return layer_norm(ALPHA * x + mix, ln_g, ln_b)


def odd_layer(x, w_in, sinks, w_out, ln_g, ln_b):
    B, S, _ = x.shape
    h = x @ w_in
    qw = GQA_Q_HEADS * HEAD_DIM
    kw = GQA_KV_HEADS * HEAD_DIM
    q, k, v = jnp.split(h, [qw, qw + kw], axis=-1)
    q = rope(q.reshape(B, S, GQA_Q_HEADS, HEAD_DIM), ROPE_THETA, ROT_DIM)
    k = rope(k.reshape(B, S, GQA_KV_HEADS, HEAD_DIM), ROPE_THETA, ROT_DIM)
    v = v.reshape(B, S, GQA_KV_HEADS, HEAD_DIM)
    mix = window_gqa_sink(q, k, v, sinks) @ w_out
    return layer_norm(ALPHA * x + mix, ln_g, ln_b)


def channel_block(x, w_up, conv_w, conv_b, w_down, ln_g, ln_b):
    S = x.shape[1]
    h = x @ w_up
    hp = jnp.pad(h, ((0, 0), (CONV_W // 2, CONV_W // 2), (0, 0)))
    h = sum(hp[:, j:j + S] * conv_w[j] for j in range(CONV_W)) + conv_b
    gate, val = jnp.split(h, 2, axis=-1)
    y = (jax.nn.gelu(gate, approximate=False) * val) @ w_down
    return layer_norm(ALPHA * x + y, ln_g, ln_b)


def encoder(x, even_params, odd_params, ffn_params):
    for layer in range(DEPTH):
        if layer % 2 == 0:
            x = even_layer(x, *even_params)
        else:
            x = odd_layer(x, *odd_params)
        x = channel_block(x, *ffn_params[layer])
    return x


def setup_inputs(seed: int = 0) -> dict:
    key = jax.random.key(seed)
    ks = iter(jax.random.split(key, 40))

    def nrm(shape, scale):
        return jax.random.normal(next(ks), shape, jnp.float32) * scale

    def gain(n):
        return 1.0 + nrm((n,), 0.01)

    d = D_MODEL
    inp = {}
    inp['x_prompt'] = nrm((BATCH, SEQ, d), 1.0)
    inp['x_sample'] = nrm((DEC_BATCH, DEC_SEQ, d), 1.0)
    inp['l0_w_in'] = nrm((d, L0_IN), d ** -0.5)
    inp['l0_rpb'] = nrm((NA_HEADS, 2 * NA_KH_MAX - 1, 2 * NA_KW - 1), 0.1)
    inp['l0_g_q_norm'] = gain(MLA_Q_RANK)
    inp['l0_w_q_up'] = nrm((MLA_Q_RANK, MLA_HEADS * (MLA_NOPE + MLA_ROPE)), MLA_Q_RANK ** -0.5)
    inp['l0_g_kv_norm'] = gain(MLA_KV_RANK)
    inp['l0_w_kv_up'] = nrm((MLA_KV_RANK, MLA_HEADS * (MLA_NOPE + MLA_V)), MLA_KV_RANK ** -0.5)
    inp['l0_w_out'] = nrm((NA_WIDTH + MLA_HEADS * MLA_V, d), BETA * (NA_WIDTH + MLA_HEADS * MLA_V) ** -0.5)
    inp['l0_ln1_g'] = gain(d)
    inp['l0_ln1_b'] = nrm((d,), 0.01)
    inp['l0_ffn_w_up'] = nrm((d, 2 * D_FF), d ** -0.5)
    inp['l0_ffn_conv_w'] = nrm((CONV_W, 2 * D_FF), CONV_W ** -0.5)
    inp['l0_ffn_conv_b'] = nrm((2 * D_FF,), 0.01)
    inp['l0_ffn_w_down'] = nrm((D_FF, d), BETA * D_FF ** -0.5)
    inp['l0_ln2_g'] = gain(d)
    inp['l0_ln2_b'] = nrm((d,), 0.01)
    inp['l1_w_in'] = nrm((d, L1_IN), d ** -0.5)
    inp['l1_sinks'] = nrm((GQA_Q_HEADS,), 1.0)
    inp['l1_w_out'] = nrm((GQA_Q_HEADS * HEAD_DIM, d), BETA * (GQA_Q_HEADS * HEAD_DIM) ** -0.5)
    inp['l1_ln1_g'] = gain(d)
    inp['l1_ln1_b'] = nrm((d,), 0.01)
    inp['l1_ffn_w_up'] = nrm((d, 2 * D_FF), d ** -0.5)
    inp['l1_ffn_conv_w'] = nrm((CONV_W, 2 * D_FF), CONV_W ** -0.5)
    inp['l1_ffn_conv_b'] = nrm((2 * D_FF,), 0.01)
    inp['l1_ffn_w_down'] = nrm((D_FF, d), BETA * D_FF ** -0.5)
    inp['l1_ln2_g'] = gain(d)
    inp['l1_ln2_b'] = nrm((d,), 0.01)
    return inp


def reference(x_prompt, x_sample,
              l0_w_in, l0_rpb, l0_g_q_norm, l0_w_q_up, l0_g_kv_norm, l0_w_kv_up, l0_w_out,
              l0_ln1_g, l0_ln1_b, l0_ffn_w_up, l0_ffn_conv_w, l0_ffn_conv_b, l0_ffn_w_down,
              l0_ln2_g, l0_ln2_b,
              l1_w_in, l1_sinks, l1_w_out, l1_ln1_g, l1_ln1_b,
              l1_ffn_w_up, l1_ffn_conv_w, l1_ffn_conv_b, l1_ffn_w_down, l1_ln2_g, l1_ln2_b):
    even_params = (l0_w_in, l0_rpb, l0_g_q_norm, l0_w_q_up, l0_g_kv_norm, l0_w_kv_up,
                   l0_w_out, l0_ln1_g, l0_ln1_b)
    odd_params = (l1_w_in, l1_sinks, l1_w_out, l1_ln1_g, l1_ln1_b)
    ffn_params = [
        (l0_ffn_w_up, l0_ffn_conv_w, l0_ffn_conv_b, l0_ffn_w_down, l0_ln2_g, l0_ln2_b),
        (l1_ffn_w_up, l1_ffn_conv_w, l1_ffn_conv_b, l1_ffn_w_down, l1_ln2_g, l1_ln2_b),
    ]
    y_prompt = encoder(x_prompt, even_params, odd_params, ffn_params)
    y_sample = encoder(x_sample, even_params, odd_params, ffn_params)
    return (y_prompt, y_sample)
```

```python
import functools

import numpy as np
import jax
import jax.numpy as jnp
from jax import lax
from jax.experimental import pallas as pl
from jax.experimental.pallas import tpu as pltpu

F32 = jnp.float32
BF16 = jnp.bfloat16

D_MODEL = 1024
DEPTH = 2
GRID_W = 64
HEAD_DIM = 64
NA_HEADS = 8
NA_KH = 8
NA_KW = 16
MLA_HEADS = 8
MLA_Q_RANK = 384
MLA_KV_RANK = 256
MLA_NOPE = 64
MLA_ROPE = 32
MLA_V = 64
MLA_THETA = 10000.0
GQA_Q_HEADS = 16
GQA_KV_HEADS = 4
WINDOW = 128
ROPE_THETA = 500000.0
ROT_DIM = HEAD_DIM // 4
D_FF = 2816
ALPHA = (2 * DEPTH) ** 0.25
NORM_EPS = 1e-5
NA_WIDTH = NA_HEADS * HEAD_DIM

LANES = 128
NEG = -1e30
VMEM_LIMIT = 56 * 1024 * 1024

PROJ_TM = 512
OUT_TM = 512
FFN_TM = 1024
FFN_CK = 256
FFN_HALO = 16
MLA_TQ = 512
NA_ROWS = 8
NA_WIN = 16
WIN_TQ = 256

_NT = (((1,), (1,)), ((), ()))


def _params(sem):
    return pltpu.CompilerParams(dimension_semantics=sem, vmem_limit_bytes=VMEM_LIMIT)


def _full(shape):
    n = len(shape)
    return pl.BlockSpec(shape, lambda *_: (0,) * n)


def _layer_norm(z, g, b):
    mu = jnp.mean(z, axis=-1, keepdims=True)
    zc = z - mu
    var = jnp.mean(zc * zc, axis=-1, keepdims=True)
    return zc * lax.rsqrt(var + NORM_EPS) * g + b


def _rms_norm(x, g):
    return x * lax.rsqrt(jnp.mean(x * x, axis=-1, keepdims=True) + NORM_EPS) * g


def _l0_proj_kernel(x_ref, wn_ref, wt_ref, gq_ref, gkv_ref, wqup_ref, wknt_ref, wvm_ref,
                    cq_ref, sq_ref, ckt_ref, skt_ref,
                    qa_ref, kat_ref, va_ref, qm_ref, kct_ref, vm_ref):
    xb = x_ref[0].astype(BF16)
    hn = jnp.dot(xb, wn_ref[...], preferred_element_type=F32)
    ht = lax.dot_general(wt_ref[...], xb, _NT, preferred_element_type=F32)
    qa_ref[0] = (hn[:, 0:NA_WIDTH] * (HEAD_DIM ** -0.5)).astype(BF16)
    va_ref[0] = hn[:, NA_WIDTH:2 * NA_WIDTH].astype(BF16)
    kat_ref[0] = ht[0:NA_WIDTH].astype(BF16)
    o = 2 * NA_WIDTH
    qn = _rms_norm(hn[:, o:o + MLA_Q_RANK], gq_ref[...]).astype(BF16)
    o += MLA_Q_RANK
    kvn = _rms_norm(hn[:, o:o + MLA_KV_RANK], gkv_ref[...]).astype(BF16)
    qq = jnp.dot(qn, wqup_ref[...], preferred_element_type=F32)
    cq = cq_ref[...]
    sq = sq_ref[...]
    hw = MLA_HEADS * LANES
    for h in range(MLA_HEADS):
        a = qq[:, h * LANES:(h + 1) * LANES]
        b = qq[:, hw + h * LANES:hw + (h + 1) * LANES]
        qm_ref[0, :, h * LANES:(h + 1) * LANES] = (a * cq + b * sq).astype(BF16)
    knt = lax.dot_general(wknt_ref[...], kvn, _NT, preferred_element_type=F32)
    r0 = NA_WIDTH
    krt = (ht[r0:r0 + MLA_ROPE] * ckt_ref[...] + ht[r0 + MLA_ROPE:r0 + 2 * MLA_ROPE] * skt_ref[...]).astype(BF16)
    tm = krt.shape[1]
    zpad = jnp.zeros((LANES - MLA_NOPE - MLA_ROPE, tm), BF16)
    for h in range(MLA_HEADS):
        kct_ref[0, h, 0, 0:MLA_NOPE, :] = knt[h * MLA_NOPE:(h + 1) * MLA_NOPE].astype(BF16)
        kct_ref[0, h, 0, MLA_NOPE:MLA_NOPE + MLA_ROPE, :] = krt
        kct_ref[0, h, 0, MLA_NOPE + MLA_ROPE:LANES, :] = zpad
    lane = lax.broadcasted_iota(jnp.int32, (tm, hw), 1)
    ones = jnp.where(lane % LANES == MLA_V, 1.0, 0.0).astype(F32)
    vm_ref[0] = (jnp.dot(kvn, wvm_ref[...], preferred_element_type=F32) + ones).astype(BF16)


def _l0_proj(x, w, tabs):
    B, S, D = x.shape
    tm = PROJ_TM
    nt = S // tm
    hw = MLA_HEADS * LANES
    cq, sq, ckt, skt = tabs
    row = lambda b, i: (b, i, 0)
    out_shape = (
        jax.ShapeDtypeStruct((B, S, NA_WIDTH), BF16),
        jax.ShapeDtypeStruct((B, NA_WIDTH, S), BF16),
        jax.ShapeDtypeStruct((B, S, NA_WIDTH), BF16),
        jax.ShapeDtypeStruct((B, S, hw), BF16),
        jax.ShapeDtypeStruct((B, MLA_HEADS, nt, LANES, tm), BF16),
        jax.ShapeDtypeStruct((B, S, hw), BF16),
    )
    return pl.pallas_call(
        _l0_proj_kernel,
        grid=(B, nt),
        in_specs=[
            pl.BlockSpec((1, tm, D), row),
            _full(w['wn'].shape), _full(w['wt'].shape), _full(w['gq'].shape), _full(w['gkv'].shape),
            _full(w['wqup'].shape), _full(w['wknt'].shape), _full(w['wvm'].shape),
            pl.BlockSpec((tm, LANES), lambda b, i: (i, 0)),
            pl.BlockSpec((tm, LANES), lambda b, i: (i, 0)),
            pl.BlockSpec((MLA_ROPE, tm), lambda b, i: (0, i)),
            pl.BlockSpec((MLA_ROPE, tm), lambda b, i: (0, i)),
        ],
        out_specs=(
            pl.BlockSpec((1, tm, NA_WIDTH), row),
            pl.BlockSpec((1, NA_WIDTH, tm), lambda b, i: (b, 0, i)),
            pl.BlockSpec((1, tm, NA_WIDTH), row),
            pl.BlockSpec((1, tm, hw), row),
            pl.BlockSpec((1, MLA_HEADS, 1, LANES, tm), lambda b, i: (b, 0, i, 0, 0)),
            pl.BlockSpec((1, tm, hw), row),
        ),
        out_shape=out_shape,
        compiler_params=_params(("parallel", "parallel")),
        name="l0_proj",
    )(x, w['wn'], w['wt'], w['gq'], w['gkv'], w['wqup'], w['wknt'], w['wvm'], cq, sq, ckt, skt)


def _na_kernel(q_ref, k0_ref, k1_ref, k2_ref, k3_ref, v0_ref, v1_ref, v2_ref, v3_ref, tab_ref, o_ref):
    q2 = q_ref[0]
    ks = (k0_ref[0], k1_ref[0], k2_ref[0], k3_ref[0])
    vs = (v0_ref[0], v1_ref[0], v2_ref[0], v3_ref[0])
    nq = q2.shape[0]
    tkp = ks[0].shape[1]
    upper = lax.broadcasted_iota(jnp.int32, (nq, LANES), 1) >= HEAD_DIM
    outs = []
    for hh in range(2):
        qh = jnp.where(upper == (hh == 1), q2, jnp.zeros_like(q2))
        s = [jnp.dot(qh, ks[t], preferred_element_type=F32) + tab_ref[0, hh, :, t * tkp:(t + 1) * tkp]
             for t in range(4)]
        m = functools.reduce(jnp.maximum, [jnp.max(st, axis=1, keepdims=True) for st in s])
        p = [jnp.exp(st - m) for st in s]
        l = functools.reduce(jnp.add, [jnp.sum(pt, axis=1, keepdims=True) for pt in p])
        o = functools.reduce(jnp.add, [jnp.dot(p[t].astype(BF16), vs[t], preferred_element_type=F32)
                                       for t in range(4)])
        outs.append(o / l)
    o_ref[0] = jnp.where(upper, outs[1], outs[0]).astype(BF16)


def _na_attention(qa, kat, va, tab):
    B, S, _ = qa.shape
    rows = S // GRID_W
    nblk = rows // NA_ROWS
    tq = NA_ROWS * GRID_W
    tkp = (NA_WIN // 4) * GRID_W
    npieces = S // tkp
    hp_n = NA_HEADS // 2

    def wstart(r):
        return jnp.clip(2 * r - 1, 0, npieces - 4)

    def case(r):
        return jnp.where(r == 0, 0, jnp.where(r == nblk - 1, 2, 1))

    k_specs = [pl.BlockSpec((1, LANES, tkp), lambda hp, b, r, t=t: (b, hp, wstart(r) + t)) for t in range(4)]
    v_specs = [pl.BlockSpec((1, tkp, LANES), lambda hp, b, r, t=t: (b, wstart(r) + t, hp)) for t in range(4)]
    return pl.pallas_call(
        _na_kernel,
        grid=(hp_n, B, nblk),
        in_specs=[pl.BlockSpec((1, tq, LANES), lambda hp, b, r: (b, r, hp))] + k_specs + v_specs + [
            pl.BlockSpec((1, 2, tq, 4 * tkp), lambda hp, b, r: (case(r), hp, 0, 0))],
        out_specs=pl.BlockSpec((1, tq, LANES), lambda hp, b, r: (b, r, hp)),
        out_shape=jax.ShapeDtypeStruct((B, S, NA_WIDTH), BF16),
        compiler_params=_params(("parallel", "parallel", "parallel")),
        name="na_attn",
    )(qa, kat, kat, kat, kat, va, va, va, va, tab)


def _na_bias_table(rpb):
    i = np.arange(NA_ROWS)[:, None]
    j = np.arange(NA_WIN)[None, :]
    first_key = np.stack([np.maximum(i - NA_KH // 2, 0) + 0 * j,
                          i + 0 * j,
                          NA_ROWS + np.minimum(i - NA_KH // 2, 0) + 0 * j])
    dr = np.stack([j - i, j - NA_KH // 2 - i, j - NA_ROWS - i])
    row_ok = (j[None] >= first_key) & (j[None] < first_key + NA_KH)
    cq = np.arange(GRID_W)[:, None]
    kc = np.arange(GRID_W)[None, :]
    cs = np.clip(cq - NA_KW // 2, 0, GRID_W - NA_KW)
    col_ok = (kc >= cs) & (kc < cs + NA_KW)
    dc = kc - cq
    ok = row_ok[:, :, None, :, None] & col_ok[None, None, :, None, :]
    dri = np.clip(dr + NA_KH - 1, 0, 2 * NA_KH - 2)[:, :, None, :, None]
    dci = np.clip(dc + NA_KW - 1, 0, 2 * NA_KW - 2)[None, None, :, None, :]
    dri, dci = np.broadcast_arrays(dri, dci)
    vals = rpb.astype(F32)[:, dri, dci]
    tab = jnp.where(ok[None], vals, NEG)
    H = rpb.shape[0]
    return tab.transpose(1, 0, 2, 3, 4, 5).reshape(3, H, NA_ROWS * GRID_W, NA_WIN * GRID_W)


def _mla_kernel(q_ref, k_ref, v_ref, o_ref, m_scr, acc_scr, *, nk, tk, c):
    q = q_ref[0]
    m_scr[...] = jnp.full(m_scr.shape, -jnp.inf, F32)
    acc_scr[...] = jnp.zeros(acc_scr.shape, F32)

    def body(kc, carry):
        kt = k_ref[0, 0, kc]
        s = jnp.dot(q, kt, preferred_element_type=F32)
        m_old = m_scr[...]
        m_new = jnp.maximum(m_old, jnp.max(s, axis=1, keepdims=True))
        alpha = jnp.exp2((m_old - m_new) * c)
        p = jnp.exp2((s - pltpu.repeat(m_new, tk // LANES, axis=1)) * c)
        v = v_ref[0, pl.ds(pl.multiple_of(kc * tk, tk), tk), :]
        acc_scr[...] = acc_scr[...] * alpha + jnp.dot(p.astype(BF16), v, preferred_element_type=F32)
        m_scr[...] = m_new
        return carry

    lax.fori_loop(0, nk, body, 0)
    acc = acc_scr[...]
    o_ref[0] = (acc / acc[:, MLA_V:MLA_V + 1]).astype(BF16)


def _mla_attention(qm, kct, vm):
    B, S, hw = qm.shape
    nk, tk = kct.shape[2], kct.shape[4]
    tq = MLA_TQ
    c = float((MLA_NOPE + MLA_ROPE) ** -0.5 * np.log2(np.e))
    return pl.pallas_call(
        functools.partial(_mla_kernel, nk=nk, tk=tk, c=c),
        grid=(B, MLA_HEADS, S // tq),
        in_specs=[
            pl.BlockSpec((1, tq, LANES), lambda b, h, i: (b, i, h)),
            pl.BlockSpec((1, 1, nk, LANES, tk), lambda b, h, i: (b, h, 0, 0, 0)),
            pl.BlockSpec((1, S, LANES), lambda b, h, i: (b, 0, h)),
        ],
        out_specs=pl.BlockSpec((1, tq, LANES), lambda b, h, i: (b, i, h)),
        out_shape=jax.ShapeDtypeStruct((B, S, hw), BF16),
        scratch_shapes=[pltpu.VMEM((tq, LANES), F32), pltpu.VMEM((tq, LANES), F32)],
        compiler_params=_params(("parallel", "parallel", "parallel")),
        name="mla_attn",
    )(qm, kct, vm)


def _out_proj_kernel(*refs, n_act):
    x_ref = refs[0]
    a_refs = refs[1:1 + n_act]
    w_refs = refs[1 + n_act:1 + 2 * n_act]
    g_ref, b_ref, o_ref = refs[1 + 2 * n_act:]
    mix = functools.reduce(jnp.add, [jnp.dot(a[...], w[...], preferred_element_type=F32)
                                     for a, w in zip(a_refs, w_refs)])
    o_ref[...] = _layer_norm(ALPHA * x_ref[...] + mix, g_ref[...], b_ref[...])


def _out_proj(x2, acts, ws, g, b):
    N, D = x2.shape
    tm = OUT_TM
    row = lambda i: (i, 0)
    return pl.pallas_call(
        functools.partial(_out_proj_kernel, n_act=len(acts)),
        grid=(N // tm,),
        in_specs=[pl.BlockSpec((tm, D), row)] + [pl.BlockSpec((tm, a.shape[1]), row) for a in acts]
        + [_full(w.shape) for w in ws] + [_full(g.shape), _full(b.shape)],
        out_specs=pl.BlockSpec((tm, D), row),
        out_shape=jax.ShapeDtypeStruct((N, D), F32),
        compiler_params=_params(("parallel",)),
        name="out_proj",
    )(x2, *acts, *ws, g, b)


def _ffn_kernel(x_ref, xp_ref, xn_ref, wup_ref, cw_ref, cb_ref, wdn_ref, g_ref, b_ref, o_ref, xb_scr, acc_scr):
    i = pl.program_id(1)
    j = pl.program_id(2)
    tm = x_ref.shape[1]
    hl = FFN_HALO

    @pl.when(j == 0)
    def _():
        xp = jnp.where(i > 0, xp_ref[0], jnp.zeros_like(xp_ref[0]))
        xn = jnp.where(i < pl.num_programs(1) - 1, xn_ref[0], jnp.zeros_like(xn_ref[0]))
        xb_scr[0:hl, :] = xp.astype(BF16)
        xb_scr[hl:hl + tm, :] = x_ref[0].astype(BF16)
        xb_scr[hl + tm:hl + tm + hl, :] = xn.astype(BF16)
        acc_scr[...] = jnp.zeros(acc_scr.shape, F32)

    h = jnp.dot(xb_scr[...], wup_ref[...], preferred_element_type=F32)
    cw = cw_ref[...]
    hc = (h[hl - 1:hl - 1 + tm] * cw[0:1] + h[hl:hl + tm] * cw[1:2] + h[hl + 1:hl + 1 + tm] * cw[2:3]
          + cb_ref[...])
    ck = hc.shape[1] // 2
    gate = hc[:, :ck]
    val = hc[:, ck:]
    act = 0.5 * gate * (1.0 + lax.erf(gate * (2.0 ** -0.5))) * val
    acc_scr[...] += jnp.dot(act.astype(BF16), wdn_ref[...], preferred_element_type=F32)

    @pl.when(j == pl.num_programs(2) - 1)
    def _():
        o_ref[0] = _layer_norm(ALPHA * x_ref[0] + acc_scr[...], g_ref[...], b_ref[...])


def _ffn(x, w):
    B, S, D = x.shape
    tm = min(FFN_TM, S)
    hl = FFN_HALO
    ck = FFN_CK
    nj = D_FF // ck
    nh = S // hl
    r = tm // hl
    return pl.pallas_call(
        _ffn_kernel,
        grid=(B, S // tm, nj),
        in_specs=[
            pl.BlockSpec((1, tm, D), lambda b, i, j: (b, i, 0)),
            pl.BlockSpec((1, hl, D), lambda b, i, j: (b, jnp.maximum(i * r - 1, 0), 0)),
            pl.BlockSpec((1, hl, D), lambda b, i, j: (b, jnp.minimum((i + 1) * r, nh - 1), 0)),
            pl.BlockSpec((D, 2 * ck), lambda b, i, j: (0, j)),
            pl.BlockSpec((3, 2 * ck), lambda b, i, j: (0, j)),
            pl.BlockSpec((1, 2 * ck), lambda b, i, j: (0, j)),
            pl.BlockSpec((ck, D), lambda b, i, j: (j, 0)),
            _full(w['g'].shape), _full(w['b'].shape),
        ],
        out_specs=pl.BlockSpec((1, tm, D), lambda b, i, j: (b, i, 0)),
        out_shape=jax.ShapeDtypeStruct((B, S, D), F32),
        scratch_shapes=[pltpu.VMEM((tm + 2 * hl, D), BF16), pltpu.VMEM((tm, D), F32)],
        compiler_params=_params(("parallel", "parallel", "arbitrary")),
        name="ffn",
    )(x, x, x, w['wup'], w['cw'], w['cb'], w['wdn'], w['g'], w['b'])


def _l1_proj_kernel(x_ref, wn_ref, wt_ref, cq_ref, sq_ref, ckt_ref, skt_ref, q_ref, kt_ref, v_ref):
    xb = x_ref[0].astype(BF16)
    hn = jnp.dot(xb, wn_ref[...], preferred_element_type=F32)
    ht = lax.dot_general(wt_ref[...], xb, _NT, preferred_element_type=F32)
    qw = GQA_Q_HEADS * HEAD_DIM
    kw = GQA_KV_HEADS * HEAD_DIM
    cq = cq_ref[...]
    sq = sq_ref[...]
    for p in range(qw // LANES):
        a = hn[:, p * LANES:(p + 1) * LANES]
        b = hn[:, qw + p * LANES:qw + (p + 1) * LANES]
        q_ref[0, :, p * LANES:(p + 1) * LANES] = ((a * cq + b * sq) * (HEAD_DIM ** -0.5)).astype(BF16)
    v_ref[0] = hn[:, 2 * qw:].astype(BF16)
    ckt = ckt_ref[...]
    skt = skt_ref[...]
    for g in range(GQA_KV_HEADS):
        kg = (ht[g * HEAD_DIM:(g + 1) * HEAD_DIM] * ckt + ht[kw + g * HEAD_DIM:kw + (g + 1) * HEAD_DIM] * skt
              ).astype(BF16)
        kt_ref[0, g, 0:HEAD_DIM, :] = kg
        kt_ref[0, g, HEAD_DIM:LANES, :] = kg


def _l1_proj(x, w, tabs):
    B, S, D = x.shape
    tm = PROJ_TM
    cq, sq, ckt, skt = tabs
    qw = GQA_Q_HEADS * HEAD_DIM
    vw = GQA_KV_HEADS * LANES
    row = lambda b, i: (b, i, 0)
    return pl.pallas_call(
        _l1_proj_kernel,
        grid=(B, S // tm),
        in_specs=[
            pl.BlockSpec((1, tm, D), row), _full(w['wn'].shape), _full(w['wt'].shape),
            pl.BlockSpec((tm, LANES), lambda b, i: (i, 0)),
            pl.BlockSpec((tm, LANES), lambda b, i: (i, 0)),
            pl.BlockSpec((HEAD_DIM, tm), lambda b, i: (0, i)),
            pl.BlockSpec((HEAD_DIM, tm), lambda b, i: (0, i)),
        ],
        out_specs=(
            pl.BlockSpec((1, tm, qw), row),
            pl.BlockSpec((1, GQA_KV_HEADS, LANES, tm), lambda b, i: (b, 0, 0, i)),
            pl.BlockSpec((1, tm, vw), row),
        ),
        out_shape=(
            jax.ShapeDtypeStruct((B, S, qw), BF16),
            jax.ShapeDtypeStruct((B, GQA_KV_HEADS, LANES, S), BF16),
            jax.ShapeDtypeStruct((B, S, vw), BF16),
        ),
        compiler_params=_params(("parallel", "parallel")),
        name="l1_proj",
    )(x, w['wn'], w['wt'], cq, sq, ckt, skt)


def _win_kernel(sink_ref, q_ref, kp_ref, km_ref, kn_ref, vp_ref, vm_ref, vn_ref, o_ref):
    i = pl.program_id(1)
    last = pl.num_programs(1) - 1
    tq = q_ref.shape[1]
    hb = kp_ref.shape[3]
    qi = lax.broadcasted_iota(jnp.int32, (tq, hb), 0)
    ki = lax.broadcasted_iota(jnp.int32, (tq, hb), 1)
    mask_p = jnp.where((ki >= qi + (hb - WINDOW)) & (i > 0), 0.0, NEG).astype(F32)
    mask_n = jnp.where((ki <= qi - (tq - WINDOW)) & (i < last), 0.0, NEG).astype(F32)
    qm_i = lax.broadcasted_iota(jnp.int32, (tq, tq), 0)
    km_i = lax.broadcasted_iota(jnp.int32, (tq, tq), 1)
    mask_m = jnp.where(jnp.abs(qm_i - km_i) <= WINDOW, 0.0, NEG).astype(F32)
    upper = lax.broadcasted_iota(jnp.int32, (tq, LANES), 1) >= HEAD_DIM
    group = GQA_Q_HEADS // GQA_KV_HEADS
    for g in range(GQA_KV_HEADS):
        kp, km, kn = kp_ref[0, g], km_ref[0, g], kn_ref[0, g]
        vp = vp_ref[0, :, g * LANES:(g + 1) * LANES]
        vm = vm_ref[0, :, g * LANES:(g + 1) * LANES]
        vn = vn_ref[0, :, g * LANES:(g + 1) * LANES]
        for pr in range(group // 2):
            col = (g * (group // 2) + pr) * LANES
            q2 = q_ref[0, :, col:col + LANES]
            outs = []
            for hh in range(2):
                sink = sink_ref[g * group + pr * 2 + hh]
                qh = jnp.where(upper == (hh == 1), q2, jnp.zeros_like(q2))
                sp = jnp.dot(qh, kp, preferred_element_type=F32) + mask_p
                sm = jnp.dot(qh, km, preferred_element_type=F32) + mask_m
                sn = jnp.dot(qh, kn, preferred_element_type=F32) + mask_n
                m = jnp.maximum(jnp.maximum(jnp.max(sp, axis=1, keepdims=True), jnp.max(sm, axis=1, keepdims=True)),
                                jnp.maximum(jnp.max(sn, axis=1, keepdims=True), sink))
                pp, pm, pn = jnp.exp(sp - m), jnp.exp(sm - m), jnp.exp(sn - m)
                l = (jnp.sum(pp, axis=1, keepdims=True) + jnp.sum(pm, axis=1, keepdims=True)
                     + jnp.sum(pn, axis=1, keepdims=True) + jnp.exp(sink - m))
                o = (jnp.dot(pp.astype(BF16), vp, preferred_element_type=F32)
                     + jnp.dot(pm.astype(BF16), vm, preferred_element_type=F32)
                     + jnp.dot(pn.astype(BF16), vn, preferred_element_type=F32))
                outs.append(o / l)
            o_ref[0, :, col:col + LANES] = jnp.where(upper, outs[1], outs[0]).astype(BF16)


def _win_attention(q, kt, v, sinks):
    B, S, qw = q.shape
    tq = min(WIN_TQ, S)
    hb = WINDOW
    r = tq // hb
    nhb = S // hb
    vw = v.shape[2]
    prev = lambda i: jnp.maximum(i * r - 1, 0)
    nxt = lambda i: jnp.minimum((i + 1) * r, nhb - 1)
    return pl.pallas_call(
        _win_kernel,
        grid_spec=pltpu.PrefetchScalarGridSpec(
            num_scalar_prefetch=1,
            grid=(B, S // tq),
            in_specs=[
                pl.BlockSpec((1, tq, qw), lambda b, i, s: (b, i, 0)),
                pl.BlockSpec((1, GQA_KV_HEADS, LANES, hb), lambda b, i, s: (b, 0, 0, prev(i))),
                pl.BlockSpec((1, GQA_KV_HEADS, LANES, tq), lambda b, i, s: (b, 0, 0, i)),
                pl.BlockSpec((1, GQA_KV_HEADS, LANES, hb), lambda b, i, s: (b, 0, 0, nxt(i))),
                pl.BlockSpec((1, hb, vw), lambda b, i, s: (b, prev(i), 0)),
                pl.BlockSpec((1, tq, vw), lambda b, i, s: (b, i, 0)),
                pl.BlockSpec((1, hb, vw), lambda b, i, s: (b, nxt(i), 0)),
            ],
            out_specs=pl.BlockSpec((1, tq, qw), lambda b, i, s: (b, i, 0)),
        ),
        out_shape=jax.ShapeDtypeStruct((B, S, qw), BF16),
        compiler_params=_params(("parallel", "parallel")),
        name="win_attn",
    )(sinks, q, kt, kt, kt, v, v, v)


def _rope_tables(S, theta, rot_dim):
    half = rot_dim // 2
    inv = 1.0 / (theta ** (jnp.arange(0, rot_dim, 2, dtype=F32) / rot_dim))
    ang = jnp.arange(S, dtype=F32)[:, None] * inv[None, :]
    cos, sin = jnp.cos(ang), jnp.sin(ang)
    return jnp.concatenate([cos, cos], axis=1), jnp.concatenate([-sin, sin], axis=1)


def _swap_halves_cols(w, rot_dim):
    half = rot_dim // 2
    return jnp.concatenate([w[:, half:rot_dim], w[:, :half]], axis=1)


def _prep_l0(w_in, g_q, w_q_up, g_kv, w_kv_up, w_out):
    d = w_in.shape[0]
    s = np.cumsum([0, NA_WIDTH, NA_WIDTH, NA_WIDTH, MLA_Q_RANK, MLA_KV_RANK, MLA_ROPE])
    w_qa, w_ka, w_va, w_ql, w_kvl, w_kr = [w_in[:, s[k]:s[k + 1]] for k in range(6)]
    wn = jnp.concatenate([w_qa, w_va, w_ql, w_kvl], axis=1).astype(BF16)
    wt = jnp.concatenate([w_ka, w_kr, _swap_halves_cols(w_kr, MLA_ROPE)], axis=1).T.astype(BF16)
    H = MLA_HEADS
    qd = MLA_NOPE + MLA_ROPE
    wq = w_q_up.reshape(MLA_Q_RANK, H, qd)
    pad = jnp.zeros((MLA_Q_RANK, H, LANES - qd), F32)
    wq_main = jnp.concatenate([wq, pad], axis=2)
    wq_sw = jnp.concatenate([jnp.zeros((MLA_Q_RANK, H, MLA_NOPE), F32),
                             wq[:, :, MLA_NOPE + MLA_ROPE // 2:], wq[:, :, MLA_NOPE:MLA_NOPE + MLA_ROPE // 2],
                             pad], axis=2)
    wqup = jnp.concatenate([wq_main.reshape(MLA_Q_RANK, H * LANES), wq_sw.reshape(MLA_Q_RANK, H * LANES)],
                           axis=1).astype(BF16)
    wkv = w_kv_up.reshape(MLA_KV_RANK, H, MLA_NOPE + MLA_V)
    wknt = wkv[:, :, :MLA_NOPE].reshape(MLA_KV_RANK, H * MLA_NOPE).T.astype(BF16)
    wvm = jnp.concatenate([wkv[:, :, MLA_NOPE:], jnp.zeros((MLA_KV_RANK, H, LANES - MLA_V), F32)],
                          axis=2).reshape(MLA_KV_RANK, H * LANES).astype(BF16)
    wob = jnp.concatenate([w_out[NA_WIDTH:].reshape(H, MLA_V, d), jnp.zeros((H, LANES - MLA_V, d), F32)],
                          axis=1).reshape(H * LANES, d).astype(BF16)
    return dict(wn=wn, wt=wt, gq=g_q.reshape(1, -1), gkv=g_kv.reshape(1, -1), wqup=wqup, wknt=wknt, wvm=wvm,
                woa=w_out[:NA_WIDTH].astype(BF16), wob=wob)


def _l0_tables(S):
    cosf, sins = _rope_tables(S, MLA_THETA, MLA_ROPE)
    ones = jnp.ones((S, MLA_NOPE), F32)
    zc = jnp.zeros((S, LANES - MLA_NOPE - MLA_ROPE), F32)
    cq = jnp.concatenate([ones, cosf, zc], axis=1)
    sq = jnp.concatenate([jnp.zeros((S, MLA_NOPE), F32), sins, zc], axis=1)
    return cq, sq, cosf.T, sins.T


def _prep_l1(w_in, w_out):
    qw = GQA_Q_HEADS * HEAD_DIM
    kw = GQA_KV_HEADS * HEAD_DIM
    wq, wk, wv = w_in[:, :qw], w_in[:, qw:qw + kw], w_in[:, qw + kw:]
    d = w_in.shape[0]

    def partner(w, heads):
        wh = w.reshape(d, heads, HEAD_DIM)
        half = ROT_DIM // 2
        return jnp.concatenate([wh[:, :, half:ROT_DIM], wh[:, :, :half],
                                jnp.zeros((d, heads, HEAD_DIM - ROT_DIM), F32)], axis=2).reshape(d, heads * HEAD_DIM)

    wv_dup = jnp.concatenate([wv.reshape(d, GQA_KV_HEADS, HEAD_DIM)] * 2, axis=2).reshape(d, GQA_KV_HEADS * LANES)
    wn = jnp.concatenate([wq, partner(wq, GQA_Q_HEADS), wv_dup], axis=1).astype(BF16)
    wt = jnp.concatenate([wk, partner(wk, GQA_KV_HEADS)], axis=1).T.astype(BF16)
    return dict(wn=wn, wt=wt, wo=w_out.astype(BF16))


def _l1_tables(S):
    cosf, sins = _rope_tables(S, ROPE_THETA, ROT_DIM)
    c64 = jnp.concatenate([cosf, jnp.ones((S, HEAD_DIM - ROT_DIM), F32)], axis=1)
    s64 = jnp.concatenate([sins, jnp.zeros((S, HEAD_DIM - ROT_DIM), F32)], axis=1)
    return jnp.concatenate([c64, c64], axis=1), jnp.concatenate([s64, s64], axis=1), c64.T, s64.T


def _prep_ffn(w_up, conv_w, conv_b, w_down, g, b):
    d = w_up.shape[0]
    nj = D_FF // FFN_CK

    def interleave(a):
        lead = a.shape[:-1]
        return a.reshape(*lead, 2, nj, FFN_CK).swapaxes(-3, -2).reshape(*lead, 2 * D_FF)

    return dict(wup=interleave(w_up).astype(BF16), cw=interleave(conv_w), cb=interleave(conv_b.reshape(1, -1)),
                wdn=w_down.astype(BF16), g=g.reshape(1, -1), b=b.reshape(1, -1))


def _encoder(x, p0, na_tab, ln0, f0, p1, sinks, ln1, f1):
    B, S, D = x.shape
    qa, kat, va, qm, kct, vm = _l0_proj(x, p0, _l0_tables(S))
    a_out = _na_attention(qa, kat, va, na_tab)
    b_out = _mla_attention(qm, kct, vm)
    x = _out_proj(x.reshape(B * S, D), [a_out.reshape(B * S, -1), b_out.reshape(B * S, -1)],
                  [p0['woa'], p0['wob']], *ln0).reshape(B, S, D)
    x = _ffn(x, f0)
    q, kt, v = _l1_proj(x, p1, _l1_tables(S))
    attn = _win_attention(q, kt, v, sinks)
    x = _out_proj(x.reshape(B * S, D), [attn.reshape(B * S, -1)], [p1['wo']], *ln1).reshape(B, S, D)
    return _ffn(x, f1)


def kernel(x_prompt, x_sample, l0_w_in, l0_rpb, l0_g_q_norm, l0_w_q_up, l0_g_kv_norm, l0_w_kv_up, l0_w_out, l0_ln1_g, l0_ln1_b, l0_ffn_w_up, l0_ffn_conv_w, l0_ffn_conv_b, l0_ffn_w_down, l0_ln2_g, l0_ln2_b, l1_w_in, l1_sinks, l1_w_out, l1_ln1_g, l1_ln1_b, l1_ffn_w_up, l1_ffn_conv_w, l1_ffn_conv_b, l1_ffn_w_down, l1_ln2_g, l1_ln2_b):
    p0 = _prep_l0(l0_w_in, l0_g_q_norm, l0_w_q_up, l0_g_kv_norm, l0_w_kv_up, l0_w_out)
    na_tab = _na_bias_table(l0_rpb)
    ln0 = (l0_ln1_g.reshape(1, -1), l0_ln1_b.reshape(1, -1))
    f0 = _prep_ffn(l0_ffn_w_up, l0_ffn_conv_w, l0_ffn_conv_b, l0_ffn_w_down, l0_ln2_g, l0_ln2_b)
    p1 = _prep_l1(l1_w_in, l1_w_out)
    ln1 = (l1_ln1_g.reshape(1, -1), l1_ln1_b.reshape(1, -1))
    f1 = _prep_ffn(l1_ffn_w_up, l1_ffn_conv_w, l1_ffn_conv_b, l1_ffn_w_down, l1_ln2_g, l1_ln2_b)
    args = (p0, na_tab, ln0, f0, p1, l1_sinks.astype(F32), ln1, f1)
    return (_encoder(x_prompt, *args), _encoder(x_sample, *args))
```

```python
import functools

import numpy as np
import jax
import jax.numpy as jnp
from jax import lax
from jax.experimental import pallas as pl
from jax.experimental.pallas import tpu as pltpu

F32 = jnp.float32
BF16 = jnp.bfloat16

D_MODEL = 1024
DEPTH = 2
GRID_W = 64
HEAD_DIM = 64
NA_HEADS = 8
NA_KH = 8
NA_KW = 16
MLA_HEADS = 8
MLA_Q_RANK = 384
MLA_KV_RANK = 256
MLA_NOPE = 64
MLA_ROPE = 32
MLA_V = 64
MLA_THETA = 10000.0
GQA_Q_HEADS = 16
GQA_KV_HEADS = 4
WINDOW = 128
ROPE_THETA = 500000.0
ROT_DIM = HEAD_DIM // 4
D_FF = 2816
ALPHA = (2 * DEPTH) ** 0.25
NORM_EPS = 1e-5
NA_WIDTH = NA_HEADS * HEAD_DIM

LANES = 128
BF16_ROWS = 16
NEG = -1e30
VMEM_LIMIT = 56 * 1024 * 1024

PROJ_TM = 512
OUT_TM = PROJ_TM
FFN_TM = 1024
FFN_CK = 256
FFN_HALO = BF16_ROWS
MLA_TQ = 256
MLA_NSUB = 4
MLA_UNROLL = 1
MLA_VROWS = MLA_V + BF16_ROWS
NA_ROWS = 8
NA_WIN = 16
WIN_TQ = 256

_NT = (((1,), (1,)), ((), ()))
_TN = (((0,), (0,)), ((), ()))


def _params(sem):
    return pltpu.CompilerParams(dimension_semantics=sem, vmem_limit_bytes=VMEM_LIMIT)


def _full(shape):
    n = len(shape)
    return pl.BlockSpec(shape, lambda *_: (0,) * n)


def _layer_norm(z, g, b):
    mu = jnp.mean(z, axis=-1, keepdims=True)
    zc = z - mu
    var = jnp.mean(zc * zc, axis=-1, keepdims=True)
    return zc * lax.rsqrt(var + NORM_EPS) * g + b


def _rms_norm(x, g):
    return x * lax.rsqrt(jnp.mean(x * x, axis=-1, keepdims=True) + NORM_EPS) * g


def _l0_proj_kernel(x_ref, wn_ref, wkat_ref, gq_ref, gkv_ref, wqupt_ref, wkn_ref, wvt_ref,
                    cq_ref, sq_ref, cqt_ref, sqt_ref,
                    qa_ref, kat_ref, va_ref, qt_ref, kc_ref, vt_ref):
    xb = x_ref[0].astype(BF16)
    hn = jnp.dot(xb, wn_ref[...], preferred_element_type=F32)
    tm = xb.shape[0]
    qa_ref[0] = (hn[:, 0:NA_WIDTH] * (HEAD_DIM ** -0.5)).astype(BF16)
    va_ref[0] = hn[:, NA_WIDTH:2 * NA_WIDTH].astype(BF16)
    kat_ref[0] = lax.dot_general(wkat_ref[...], xb, _NT, preferred_element_type=F32).astype(BF16)
    o = 2 * NA_WIDTH
    qn = _rms_norm(hn[:, o:o + MLA_Q_RANK], gq_ref[...]).astype(BF16)
    o += MLA_Q_RANK
    kvn = _rms_norm(hn[:, o:o + MLA_KV_RANK], gkv_ref[...]).astype(BF16)
    o += MLA_KV_RANK
    kr = hn[:, o:o + LANES] * cq_ref[...] + hn[:, o + LANES:o + 2 * LANES] * sq_ref[...]
    hw = MLA_HEADS * LANES
    qqt = lax.dot_general(wqupt_ref[...], qn, _NT, preferred_element_type=F32)
    cqt = cqt_ref[...]
    sqt = sqt_ref[...]
    kn = jnp.dot(kvn, wkn_ref[...], preferred_element_type=F32)
    vt = lax.dot_general(wvt_ref[...], kvn, _NT, preferred_element_type=F32)
    ones_rows = jnp.where(lax.broadcasted_iota(jnp.int32, (BF16_ROWS, tm), 0) == 0, 1.0, 0.0).astype(BF16)
    for h in range(MLA_HEADS):
        a = qqt[h * LANES:(h + 1) * LANES]
        b = qqt[hw + h * LANES:hw + (h + 1) * LANES]
        qt_ref[0, h, 0] = (a * cqt + b * sqt).astype(BF16)
        kc_ref[0, :, h * LANES:(h + 1) * LANES] = (kn[:, h * LANES:(h + 1) * LANES] + kr).astype(BF16)
        vt_ref[0, h, 0, 0:MLA_V, :] = vt[h * MLA_V:(h + 1) * MLA_V].astype(BF16)
        vt_ref[0, h, 0, MLA_V:MLA_VROWS, :] = ones_rows


def _l0_proj(x, w, tabs):
    B, S, D = x.shape
    tm = PROJ_TM
    nt = S // tm
    hw = MLA_HEADS * LANES
    cq, sq, cqt, sqt = tabs
    row = lambda b, i: (b, i, 0)
    chunk = lambda b, i: (b, 0, i, 0, 0)
    out_shape = (
        jax.ShapeDtypeStruct((B, S, NA_WIDTH), BF16),
        jax.ShapeDtypeStruct((B, NA_WIDTH, S), BF16),
        jax.ShapeDtypeStruct((B, S, NA_WIDTH), BF16),
        jax.ShapeDtypeStruct((B, MLA_HEADS, nt, LANES, tm), BF16),
        jax.ShapeDtypeStruct((B, S, hw), BF16),
        jax.ShapeDtypeStruct((B, MLA_HEADS, nt, MLA_VROWS, tm), BF16),
    )
    return pl.pallas_call(
        _l0_proj_kernel,
        grid=(B, nt),
        in_specs=[
            pl.BlockSpec((1, tm, D), row),
            _full(w['wn'].shape), _full(w['wkat'].shape), _full(w['gq'].shape), _full(w['gkv'].shape),
            _full(w['wqupt'].shape), _full(w['wkn'].shape), _full(w['wvt'].shape),
            pl.BlockSpec((tm, LANES), lambda b, i: (i, 0)),
            pl.BlockSpec((tm, LANES), lambda b, i: (i, 0)),
            pl.BlockSpec((LANES, tm), lambda b, i: (0, i)),
            pl.BlockSpec((LANES, tm), lambda b, i: (0, i)),
        ],
        out_specs=(
            pl.BlockSpec((1, tm, NA_WIDTH), row),
            pl.BlockSpec((1, NA_WIDTH, tm), lambda b, i: (b, 0, i)),
            pl.BlockSpec((1, tm, NA_WIDTH), row),
            pl.BlockSpec((1, MLA_HEADS, 1, LANES, tm), chunk),
            pl.BlockSpec((1, tm, hw), row),
            pl.BlockSpec((1, MLA_HEADS, 1, MLA_VROWS, tm), chunk),
        ),
        out_shape=out_shape,
        compiler_params=_params(("parallel", "parallel")),
        name="l0_proj",
    )(x, w['wn'], w['wkat'], w['gq'], w['gkv'], w['wqupt'], w['wkn'], w['wvt'], cq, sq, cqt, sqt)


def _na_kernel(q_ref, k0_ref, k1_ref, k2_ref, k3_ref, v0_ref, v1_ref, v2_ref, v3_ref, y2_ref, o_ref, tab_scr):
    r = pl.program_id(2)
    nblk = pl.num_programs(2)
    case = jnp.where(r == 0, 0, jnp.where(r == nblk - 1, 2, 1))

    @pl.when((r <= 1) | (r == nblk - 1))
    def _():
        left = lax.broadcasted_iota(jnp.int32, (GRID_W, LANES), 1) < GRID_W
        for i in range(NA_ROWS):
            first_key = jnp.where(case == 0, max(i - NA_KH // 2, 0),
                                  jnp.where(case == 1, i, min(i + NA_KH // 2, NA_ROWS)))
            for jp in range(NA_WIN // 2):
                ok_l = (2 * jp >= first_key) & (2 * jp < first_key + NA_KH)
                ok_r = (2 * jp + 1 >= first_key) & (2 * jp + 1 < first_key + NA_KH)
                pen = jnp.where(left, jnp.where(ok_l, 0.0, NEG), jnp.where(ok_r, 0.0, NEG)).astype(F32)
                a = jnp.clip(2 * jp - i + NA_KH - (NA_KH // 2) * case, 0, 2 * NA_KH - 1)
                for hh in range(2):
                    tab_scr[hh, i * GRID_W:(i + 1) * GRID_W, jp * LANES:(jp + 1) * LANES] = y2_ref[hh, a] + pen

    q2 = q_ref[0]
    ks = (k0_ref[0], k1_ref[0], k2_ref[0], k3_ref[0])
    vs = (v0_ref[0], v1_ref[0], v2_ref[0], v3_ref[0])
    nq = q2.shape[0]
    tkp = ks[0].shape[1]
    upper = lax.broadcasted_iota(jnp.int32, (nq, LANES), 1) >= HEAD_DIM
    outs = []
    for hh in range(2):
        qh = jnp.where(upper == (hh == 1), q2, jnp.zeros_like(q2))
        s = [jnp.dot(qh, ks[t], preferred_element_type=F32) + tab_scr[hh, :, t * tkp:(t + 1) * tkp]
             for t in range(4)]
        m = functools.reduce(jnp.maximum, [jnp.max(st, axis=1, keepdims=True) for st in s])
        p = [jnp.exp(st - m) for st in s]
        l = functools.reduce(jnp.add, [jnp.sum(pt, axis=1, keepdims=True) for pt in p])
        o = functools.reduce(jnp.add, [jnp.dot(p[t].astype(BF16), vs[t], preferred_element_type=F32)
                                       for t in range(4)])
        outs.append(o / l)
    o_ref[0] = jnp.where(upper, outs[1], outs[0]).astype(BF16)


def _na_attention(qa, kat, va, y2):
    B, S, _ = qa.shape
    rows = S // GRID_W
    nblk = rows // NA_ROWS
    assert nblk >= 2 and rows % NA_ROWS == 0
    tq = NA_ROWS * GRID_W
    tkp = (NA_WIN // 4) * GRID_W
    npieces = S // tkp
    hp_n = NA_HEADS // 2

    def wstart(r):
        return jnp.clip(2 * r - 1, 0, npieces - 4)

    k_specs = [pl.BlockSpec((1, LANES, tkp), lambda hp, b, r, t=t: (b, hp, wstart(r) + t)) for t in range(4)]
    v_specs = [pl.BlockSpec((1, tkp, LANES), lambda hp, b, r, t=t: (b, wstart(r) + t, hp)) for t in range(4)]
    return pl.pallas_call(
        _na_kernel,
        grid=(hp_n, B, nblk),
        in_specs=[pl.BlockSpec((1, tq, LANES), lambda hp, b, r: (b, r, hp))] + k_specs + v_specs + [
            pl.BlockSpec((2, 2 * NA_KH, GRID_W, LANES), lambda hp, b, r: (hp, 0, 0, 0))],
        out_specs=pl.BlockSpec((1, tq, LANES), lambda hp, b, r: (b, r, hp)),
        out_shape=jax.ShapeDtypeStruct((B, S, NA_WIDTH), BF16),
        scratch_shapes=[pltpu.VMEM((2, tq, 4 * tkp), F32)],
        compiler_params=_params(("arbitrary", "arbitrary", "arbitrary")),
        name="na_attn",
    )(qa, kat, kat, kat, kat, va, va, va, va, y2)


def _na_bias_tiles(rpb):
    cq = np.arange(GRID_W)[:, None]
    kc = np.arange(GRID_W)[None, :]
    cs = np.clip(cq - NA_KW // 2, 0, GRID_W - NA_KW)
    col_ok = (kc >= cs) & (kc < cs + NA_KW)
    onehot = ((kc - cq + NA_KW - 1)[None] == np.arange(2 * NA_KW - 1)[:, None, None]) & col_ok[None]
    y = jnp.einsum('hab,bqk->haqk', rpb.astype(F32), jnp.asarray(onehot, F32), precision=lax.Precision.HIGHEST)
    y = jnp.where(col_ok[None, None], y, NEG)
    y = jnp.pad(y, ((0, 0), (1, 1), (0, 0), (0, 0)), constant_values=NEG)
    return jnp.concatenate([y[:, :-1], y[:, 1:]], axis=-1)


def _mla_kernel(q_ref, k_ref, v_ref, o_ref, s_scr, *, nq, nk, tk, tq, nsub, unroll, c):
    per_chunk = tk // tq

    def q_body(qi, carry):
        where = [(qi * (nsub // per_chunk) + j // per_chunk, (j % per_chunk) * tq) for j in range(nsub)]
        qts = [q_ref[0, 0, ci, :, lo:lo + tq] for ci, lo in where]

        def scores(kc, slot):
            k = k_ref[0, pl.ds(pl.multiple_of(kc * tk, tk), tk), :]
            for j, qt in enumerate(qts):
                s_scr[slot, j] = jnp.dot(k, qt, preferred_element_type=F32)

        def fold(kc, slot, mc):
            v = v_ref[0, 0, kc]
            stats = []
            for j, (m_old, _) in enumerate(mc):
                st = s_scr[slot, j]
                m_new = jnp.maximum(m_old, jnp.max(st, axis=0, keepdims=True))
                alpha = jnp.exp2((m_old - m_new) * c)
                pt = jnp.exp2((st - m_new) * c).astype(BF16)
                stats.append((m_new, alpha, pt))
            return tuple((m_new, acc * alpha + jnp.dot(v, pt, preferred_element_type=F32))
                         for (_, acc), (m_new, alpha, pt) in zip(mc, stats))

        def k_body(i, mc):
            scores(2 * i + 1, 1)
            mc = fold(2 * i, 0, mc)
            scores(2 * i + 2, 0)
            return fold(2 * i + 1, 1, mc)

        init = tuple((jnp.full((1, tq), -jnp.inf, F32), jnp.zeros((MLA_VROWS, tq), F32)) for _ in range(nsub))
        scores(0, 0)
        mc = lax.fori_loop(0, nk // 2 - 1, k_body, init, unroll=unroll)
        scores(nk - 1, 1)
        res = fold(nk - 1, 1, fold(nk - 2, 0, mc))
        for (_, acc), (ci, lo) in zip(res, where):
            o_ref[0, 0, ci, :, lo:lo + tq] = (acc[0:MLA_V] / acc[MLA_V:MLA_V + 1]).astype(BF16)
        return carry

    lax.fori_loop(0, nq, q_body, 0)


def _mla_attention(qt, kc, vt):
    B, H, nt, _, tm = qt.shape
    S = nt * tm
    c = float((MLA_NOPE + MLA_ROPE) ** -0.5 * np.log2(np.e))
    head = lambda b, h: (b, h, 0, 0, 0)
    nsub = MLA_NSUB
    assert nsub % (tm // MLA_TQ) == 0 and (nt * (tm // MLA_TQ)) % nsub == 0 and nt % 2 == 0
    return pl.pallas_call(
        functools.partial(_mla_kernel, nq=nt * (tm // MLA_TQ) // nsub, nk=nt, tk=tm, tq=MLA_TQ, nsub=nsub,
                          unroll=MLA_UNROLL, c=c),
        grid=(B, H),
        in_specs=[
            pl.BlockSpec((1, 1, nt, LANES, tm), head),
            pl.BlockSpec((1, S, LANES), lambda b, h: (b, 0, h)),
            pl.BlockSpec((1, 1, nt, MLA_VROWS, tm), head),
        ],
        out_specs=pl.BlockSpec((1, 1, nt, MLA_V, tm), head),
        out_shape=jax.ShapeDtypeStruct((B, H, nt, MLA_V, tm), BF16),
        scratch_shapes=[pltpu.VMEM((2, nsub, tm, MLA_TQ), F32)],
        compiler_params=_params(("parallel", "parallel")),
        name="mla_attn",
    )(qt, kc, vt)


def _out_proj_kernel(*refs, n_act, has_t):
    x_ref = refs[0]
    a_refs = refs[1:1 + n_act]
    w_refs = refs[1 + n_act:1 + 2 * n_act]
    g_ref, b_ref, o_ref = refs[1 + 2 * n_act:]
    mix = None
    for k, (a, w) in enumerate(zip(a_refs, w_refs)):
        if has_t and k == n_act - 1:
            at = a[0, :, 0]
            at = at.reshape(at.shape[0] * at.shape[1], at.shape[2])
            t = lax.dot_general(at, w[...], _TN, preferred_element_type=F32)
        else:
            t = jnp.dot(a[...], w[...], preferred_element_type=F32)
        mix = t if mix is None else mix + t
    o_ref[...] = _layer_norm(ALPHA * x_ref[...] + mix, g_ref[...], b_ref[...])


def _out_proj(x2, acts, ws, g, b, act_t=None, w_t=None):
    N, D = x2.shape
    tm = OUT_TM
    row = lambda i: (i, 0)
    specs = [pl.BlockSpec((tm, a.shape[1]), row) for a in acts]
    acts, ws = list(acts), list(ws)
    if act_t is not None:
        _, H, nt, hd, tmc = act_t.shape
        assert tmc == tm
        specs.append(pl.BlockSpec((1, H, 1, hd, tm), lambda i: (i // nt, 0, i % nt, 0, 0)))
        acts.append(act_t)
        ws.append(w_t)
    return pl.pallas_call(
        functools.partial(_out_proj_kernel, n_act=len(acts), has_t=act_t is not None),
        grid=(N // tm,),
        in_specs=[pl.BlockSpec((tm, D), row)] + specs + [_full(w.shape) for w in ws] + [_full(g.shape), _full(b.shape)],
        out_specs=pl.BlockSpec((tm, D), row),
        out_shape=jax.ShapeDtypeStruct((N, D), F32),
        compiler_params=_params(("parallel",)),
        name="out_proj",
    )(x2, *acts, *ws, g, b)


def _ffn_kernel(x_ref, xp_ref, xn_ref, wup_ref, cw_ref, cb_ref, wdn_ref, g_ref, b_ref, o_ref, xb_scr, acc_scr):
    i = pl.program_id(1)
    j = pl.program_id(2)
    tm = x_ref.shape[1]
    hl = FFN_HALO

    @pl.when(j == 0)
    def _():
        xp = jnp.where(i > 0, xp_ref[0], jnp.zeros_like(xp_ref[0]))
        xn = jnp.where(i < pl.num_programs(1) - 1, xn_ref[0], jnp.zeros_like(xn_ref[0]))
        xb_scr[0:hl, :] = xp.astype(BF16)
        xb_scr[hl:hl + tm, :] = x_ref[0].astype(BF16)
        xb_scr[hl + tm:hl + tm + hl, :] = xn.astype(BF16)
        acc_scr[...] = jnp.zeros(acc_scr.shape, F32)

    h = jnp.dot(xb_scr[...], wup_ref[...], preferred_element_type=F32)
    cw = cw_ref[...]
    hc = (h[hl - 1:hl - 1 + tm] * cw[0:1] + h[hl:hl + tm] * cw[1:2] + h[hl + 1:hl + 1 + tm] * cw[2:3]
          + cb_ref[...])
    ck = hc.shape[1] // 2
    gate = hc[:, :ck]
    val = hc[:, ck:]
    act = 0.5 * gate * (1.0 + lax.erf(gate * (2.0 ** -0.5))) * val
    acc_scr[...] += jnp.dot(act.astype(BF16), wdn_ref[...], preferred_element_type=F32)

    @pl.when(j == pl.num_programs(2) - 1)
    def _():
        o_ref[0] = _layer_norm(ALPHA * x_ref[0] + acc_scr[...], g_ref[...], b_ref[...])


def _ffn(x, w):
    B, S, D = x.shape
    tm = min(FFN_TM, S)
    hl = FFN_HALO
    ck = FFN_CK
    nj = D_FF // ck
    nh = S // hl
    r = tm // hl
    return pl.pallas_call(
        _ffn_kernel,
        grid=(B, S // tm, nj),
        in_specs=[
            pl.BlockSpec((1, tm, D), lambda b, i, j: (b, i, 0)),
            pl.BlockSpec((1, hl, D), lambda b, i, j: (b, jnp.maximum(i * r - 1, 0), 0)),
            pl.BlockSpec((1, hl, D), lambda b, i, j: (b, jnp.minimum((i + 1) * r, nh - 1), 0)),
            pl.BlockSpec((D, 2 * ck), lambda b, i, j: (0, j)),
            pl.BlockSpec((3, 2 * ck), lambda b, i, j: (0, j)),
            pl.BlockSpec((1, 2 * ck), lambda b, i, j: (0, j)),
            pl.BlockSpec((ck, D), lambda b, i, j: (j, 0)),
            _full(w['g'].shape), _full(w['b'].shape),
        ],
        out_specs=pl.BlockSpec((1, tm, D), lambda b, i, j: (b, i, 0)),
        out_shape=jax.ShapeDtypeStruct((B, S, D), F32),
        scratch_shapes=[pltpu.VMEM((tm + 2 * hl, D), BF16), pltpu.VMEM((tm, D), F32)],
        compiler_params=_params(("parallel", "parallel", "arbitrary")),
        name="ffn",
    )(x, x, x, w['wup'], w['cw'], w['cb'], w['wdn'], w['g'], w['b'])


def _l1_proj_kernel(x_ref, wn_ref, wt_ref, cq_ref, sq_ref, ckt_ref, skt_ref, q_ref, kt_ref, v_ref):
    xb = x_ref[0].astype(BF16)
    hn = jnp.dot(xb, wn_ref[...], preferred_element_type=F32)
    ht = lax.dot_general(wt_ref[...], xb, _NT, preferred_element_type=F32)
    qw = GQA_Q_HEADS * HEAD_DIM
    kw = GQA_KV_HEADS * HEAD_DIM
    cq = cq_ref[...]
    sq = sq_ref[...]
    for p in range(qw // LANES):
        a = hn[:, p * LANES:(p + 1) * LANES]
        b = hn[:, qw + p * LANES:qw + (p + 1) * LANES]
        q_ref[0, :, p * LANES:(p + 1) * LANES] = ((a * cq + b * sq) * (HEAD_DIM ** -0.5)).astype(BF16)
    v_ref[0] = hn[:, 2 * qw:].astype(BF16)
    ckt = ckt_ref[...]
    skt = skt_ref[...]
    for g in range(GQA_KV_HEADS):
        kg = (ht[g * HEAD_DIM:(g + 1) * HEAD_DIM] * ckt + ht[kw + g * HEAD_DIM:kw + (g + 1) * HEAD_DIM] * skt
              ).astype(BF16)
        kt_ref[0, g, 0:HEAD_DIM, :] = kg
        kt_ref[0, g, HEAD_DIM:LANES, :] = kg


def _l1_proj(x, w, tabs):
    B, S, D = x.shape
    tm = PROJ_TM
    cq, sq, ckt, skt = tabs
    qw = GQA_Q_HEADS * HEAD_DIM
    vw = GQA_KV_HEADS * LANES
    row = lambda b, i: (b, i, 0)
    return pl.pallas_call(
        _l1_proj_kernel,
        grid=(B, S // tm),
        in_specs=[
            pl.BlockSpec((1, tm, D), row), _full(w['wn'].shape), _full(w['wt'].shape),
            pl.BlockSpec((tm, LANES), lambda b, i: (i, 0)),
            pl.BlockSpec((tm, LANES), lambda b, i: (i, 0)),
            pl.BlockSpec((HEAD_DIM, tm), lambda b, i: (0, i)),
            pl.BlockSpec((HEAD_DIM, tm), lambda b, i: (0, i)),
        ],
        out_specs=(
            pl.BlockSpec((1, tm, qw), row),
            pl.BlockSpec((1, GQA_KV_HEADS, LANES, tm), lambda b, i: (b, 0, 0, i)),
            pl.BlockSpec((1, tm, vw), row),
        ),
        out_shape=(
            jax.ShapeDtypeStruct((B, S, qw), BF16),
            jax.ShapeDtypeStruct((B, GQA_KV_HEADS, LANES, S), BF16),
            jax.ShapeDtypeStruct((B, S, vw), BF16),
        ),
        compiler_params=_params(("parallel", "parallel")),
        name="l1_proj",
    )(x, w['wn'], w['wt'], cq, sq, ckt, skt)


def _win_kernel(sink_ref, q_ref, kp_ref, km_ref, kn_ref, vp_ref, vm_ref, vn_ref, o_ref):
    i = pl.program_id(1)
    last = pl.num_programs(1) - 1
    tq = q_ref.shape[1]
    hb = kp_ref.shape[3]
    qi = lax.broadcasted_iota(jnp.int32, (tq, hb), 0)
    ki = lax.broadcasted_iota(jnp.int32, (tq, hb), 1)
    mask_p = jnp.where((ki >= qi + (hb - WINDOW)) & (i > 0), 0.0, NEG).astype(F32)
    mask_n = jnp.where((ki <= qi - (tq - WINDOW)) & (i < last), 0.0, NEG).astype(F32)
    qm_i = lax.broadcasted_iota(jnp.int32, (tq, tq), 0)
    km_i = lax.broadcasted_iota(jnp.int32, (tq, tq), 1)
    mask_m = jnp.where(jnp.abs(qm_i - km_i) <= WINDOW, 0.0, NEG).astype(F32)
    upper = lax.broadcasted_iota(jnp.int32, (tq, LANES), 1) >= HEAD_DIM
    group = GQA_Q_HEADS // GQA_KV_HEADS
    for g in range(GQA_KV_HEADS):
        kp, km, kn = kp_ref[0, g], km_ref[0, g], kn_ref[0, g]
        vp = vp_ref[0, :, g * LANES:(g + 1) * LANES]
        vm = vm_ref[0, :, g * LANES:(g + 1) * LANES]
        vn = vn_ref[0, :, g * LANES:(g + 1) * LANES]
        for pr in range(group // 2):
            col = (g * (group // 2) + pr) * LANES
            q2 = q_ref[0, :, col:col + LANES]
            outs = []
            for hh in range(2):
                sink = sink_ref[g * group + pr * 2 + hh]
                qh = jnp.where(upper == (hh == 1), q2, jnp.zeros_like(q2))
                sp = jnp.dot(qh, kp, preferred_element_type=F32) + mask_p
                sm = jnp.dot(qh, km, preferred_element_type=F32) + mask_m
                sn = jnp.dot(qh, kn, preferred_element_type=F32) + mask_n
                m = jnp.maximum(jnp.maximum(jnp.max(sp, axis=1, keepdims=True), jnp.max(sm, axis=1, keepdims=True)),
                                jnp.maximum(jnp.max(sn, axis=1, keepdims=True), sink))
                pp, pm, pn = jnp.exp(sp - m), jnp.exp(sm - m), jnp.exp(sn - m)
                l = (jnp.sum(pp, axis=1, keepdims=True) + jnp.sum(pm, axis=1, keepdims=True)
                     + jnp.sum(pn, axis=1, keepdims=True) + jnp.exp(sink - m))
                o = (jnp.dot(pp.astype(BF16), vp, preferred_element_type=F32)
                     + jnp.dot(pm.astype(BF16), vm, preferred_element_type=F32)
                     + jnp.dot(pn.astype(BF16), vn, preferred_element_type=F32))
                outs.append(o / l)
            o_ref[0, :, col:col + LANES] = jnp.where(upper, outs[1], outs[0]).astype(BF16)


def _win_attention(q, kt, v, sinks):
    B, S, qw = q.shape
    tq = min(WIN_TQ, S)
    hb = WINDOW
    r = tq // hb
    nhb = S // hb
    vw = v.shape[2]
    prev = lambda i: jnp.maximum(i * r - 1, 0)
    nxt = lambda i: jnp.minimum((i + 1) * r, nhb - 1)
    return pl.pallas_call(
        _win_kernel,
        grid_spec=pltpu.PrefetchScalarGridSpec(
            num_scalar_prefetch=1,
            grid=(B, S // tq),
            in_specs=[
                pl.BlockSpec((1, tq, qw), lambda b, i, s: (b, i, 0)),
                pl.BlockSpec((1, GQA_KV_HEADS, LANES, hb), lambda b, i, s: (b, 0, 0, prev(i))),
                pl.BlockSpec((1, GQA_KV_HEADS, LANES, tq), lambda b, i, s: (b, 0, 0, i)),
                pl.BlockSpec((1, GQA_KV_HEADS, LANES, hb), lambda b, i, s: (b, 0, 0, nxt(i))),
                pl.BlockSpec((1, hb, vw), lambda b, i, s: (b, prev(i), 0)),
                pl.BlockSpec((1, tq, vw), lambda b, i, s: (b, i, 0)),
                pl.BlockSpec((1, hb, vw), lambda b, i, s: (b, nxt(i), 0)),
            ],
            out_specs=pl.BlockSpec((1, tq, qw), lambda b, i, s: (b, i, 0)),
        ),
        out_shape=jax.ShapeDtypeStruct((B, S, qw), BF16),
        compiler_params=_params(("parallel", "parallel")),
        name="win_attn",
    )(sinks, q, kt, kt, kt, v, v, v)


def _rope_tables(S, theta, rot_dim):
    inv = 1.0 / (theta ** (jnp.arange(0, rot_dim, 2, dtype=F32) / rot_dim))
    ang = jnp.arange(S, dtype=F32)[:, None] * inv[None, :]
    cos, sin = jnp.cos(ang), jnp.sin(ang)
    return jnp.concatenate([cos, cos], axis=1), jnp.concatenate([-sin, sin], axis=1)


def _swap_halves_cols(w, rot_dim):
    half = rot_dim // 2
    return jnp.concatenate([w[:, half:rot_dim], w[:, :half]], axis=1)


def _prep_l0(w_in, g_q, w_q_up, g_kv, w_kv_up, w_out):
    d = w_in.shape[0]
    s = np.cumsum([0, NA_WIDTH, NA_WIDTH, NA_WIDTH, MLA_Q_RANK, MLA_KV_RANK, MLA_ROPE])
    w_qa, w_ka, w_va, w_ql, w_kvl, w_kr = [w_in[:, s[k]:s[k + 1]] for k in range(6)]
    H = MLA_HEADS
    qd = MLA_NOPE + MLA_ROPE

    def rope_block(w):
        return jnp.concatenate([jnp.zeros((d, MLA_NOPE), F32), w, jnp.zeros((d, LANES - qd), F32)], axis=1)

    wn = jnp.concatenate([w_qa, w_va, w_ql, w_kvl, rope_block(w_kr), rope_block(_swap_halves_cols(w_kr, MLA_ROPE))],
                         axis=1).astype(BF16)
    wq = w_q_up.reshape(MLA_Q_RANK, H, qd)
    pad = jnp.zeros((MLA_Q_RANK, H, LANES - qd), F32)
    wq_main = jnp.concatenate([wq, pad], axis=2)
    wq_sw = jnp.concatenate([jnp.zeros((MLA_Q_RANK, H, MLA_NOPE), F32),
                             wq[:, :, MLA_NOPE + MLA_ROPE // 2:], wq[:, :, MLA_NOPE:MLA_NOPE + MLA_ROPE // 2],
                             pad], axis=2)
    wqupt = jnp.concatenate([wq_main.reshape(MLA_Q_RANK, H * LANES), wq_sw.reshape(MLA_Q_RANK, H * LANES)],
                            axis=1).T.astype(BF16)
    wkv = w_kv_up.reshape(MLA_KV_RANK, H, MLA_NOPE + MLA_V)
    wkn = jnp.concatenate([wkv[:, :, :MLA_NOPE], jnp.zeros((MLA_KV_RANK, H, LANES - MLA_NOPE), F32)],
                          axis=2).reshape(MLA_KV_RANK, H * LANES).astype(BF16)
    wvt = wkv[:, :, MLA_NOPE:].reshape(MLA_KV_RANK, H * MLA_V).T.astype(BF16)
    return dict(wn=wn, wkat=w_ka.T.astype(BF16), gq=g_q.reshape(1, -1), gkv=g_kv.reshape(1, -1), wqupt=wqupt,
                wkn=wkn, wvt=wvt, woa=w_out[:NA_WIDTH].astype(BF16), wob=w_out[NA_WIDTH:].astype(BF16))


def _l0_tables(S):
    cosf, sins = _rope_tables(S, MLA_THETA, MLA_ROPE)
    ones = jnp.ones((S, MLA_NOPE), F32)
    zc = jnp.zeros((S, LANES - MLA_NOPE - MLA_ROPE), F32)
    cq = jnp.concatenate([ones, cosf, zc], axis=1)
    sq = jnp.concatenate([jnp.zeros((S, MLA_NOPE), F32), sins, zc], axis=1)
    return cq, sq, cq.T, sq.T


def _prep_l1(w_in, w_out):
    qw = GQA_Q_HEADS * HEAD_DIM
    kw = GQA_KV_HEADS * HEAD_DIM
    wq, wk, wv = w_in[:, :qw], w_in[:, qw:qw + kw], w_in[:, qw + kw:]
    d = w_in.shape[0]

    def partner(w, heads):
        wh = w.reshape(d, heads, HEAD_DIM)
        half = ROT_DIM // 2
        return jnp.concatenate([wh[:, :, half:ROT_DIM], wh[:, :, :half],
                                jnp.zeros((d, heads, HEAD_DIM - ROT_DIM), F32)], axis=2).reshape(d, heads * HEAD_DIM)

    wv_dup = jnp.concatenate([wv.reshape(d, GQA_KV_HEADS, HEAD_DIM)] * 2, axis=2).reshape(d, GQA_KV_HEADS * LANES)
    wn = jnp.concatenate([wq, partner(wq, GQA_Q_HEADS), wv_dup], axis=1).astype(BF16)
    wt = jnp.concatenate([wk, partner(wk, GQA_KV_HEADS)], axis=1).T.astype(BF16)
    return dict(wn=wn, wt=wt, wo=w_out.astype(BF16))


def _l1_tables(S):
    cosf, sins = _rope_tables(S, ROPE_THETA, ROT_DIM)
    c64 = jnp.concatenate([cosf, jnp.ones((S, HEAD_DIM - ROT_DIM), F32)], axis=1)
    s64 = jnp.concatenate([sins, jnp.zeros((S, HEAD_DIM - ROT_DIM), F32)], axis=1)
    return jnp.concatenate([c64, c64], axis=1), jnp.concatenate([s64, s64], axis=1), c64.T, s64.T


def _prep_ffn(w_up, conv_w, conv_b, w_down, g, b):
    nj = D_FF // FFN_CK

    def interleave(a):
        lead = a.shape[:-1]
        return a.reshape(*lead, 2, nj, FFN_CK).swapaxes(-3, -2).reshape(*lead, 2 * D_FF)

    return dict(wup=interleave(w_up).astype(BF16), cw=interleave(conv_w), cb=interleave(conv_b.reshape(1, -1)),
                wdn=w_down.astype(BF16), g=g.reshape(1, -1), b=b.reshape(1, -1))


def _encoder(x, p0, na_tiles, ln0, f0, p1, sinks, ln1, f1):
    B, S, D = x.shape
    qa, kat, va, qt, kc, vt = _l0_proj(x, p0, _l0_tables(S))
    a_out = _na_attention(qa, kat, va, na_tiles)
    b_out_t = _mla_attention(qt, kc, vt)
    x = _out_proj(x.reshape(B * S, D), [a_out.reshape(B * S, -1)], [p0['woa']], *ln0,
                  act_t=b_out_t, w_t=p0['wob']).reshape(B, S, D)
    x = _ffn(x, f0)
    q, kt, v = _l1_proj(x, p1, _l1_tables(S))
    attn = _win_attention(q, kt, v, sinks)
    x = _out_proj(x.reshape(B * S, D), [attn.reshape(B * S, -1)], [p1['wo']], *ln1).reshape(B, S, D)
    return _ffn(x, f1)


def kernel(x_prompt, x_sample, l0_w_in, l0_rpb, l0_g_q_norm, l0_w_q_up, l0_g_kv_norm, l0_w_kv_up, l0_w_out, l0_ln1_g, l0_ln1_b, l0_ffn_w_up, l0_ffn_conv_w, l0_ffn_conv_b, l0_ffn_w_down, l0_ln2_g, l0_ln2_b, l1_w_in, l1_sinks, l1_w_out, l1_ln1_g, l1_ln1_b, l1_ffn_w_up, l1_ffn_conv_w, l1_ffn_conv_b, l1_ffn_w_down, l1_ln2_g, l1_ln2_b):
    p0 = _prep_l0(l0_w_in, l0_g_q_norm, l0_w_q_up, l0_g_kv_norm, l0_w_kv_up, l0_w_out)
    na_tiles = _na_bias_tiles(l0_rpb)
    ln0 = (l0_ln1_g.reshape(1, -1), l0_ln1_b.reshape(1, -1))
    f0 = _prep_ffn(l0_ffn_w_up, l0_ffn_conv_w, l0_ffn_conv_b, l0_ffn_w_down, l0_ln2_g, l0_ln2_b)
    p1 = _prep_l1(l1_w_in, l1_w_out)
    ln1 = (l1_ln1_g.reshape(1, -1), l1_ln1_b.reshape(1, -1))
    f1 = _prep_ffn(l1_ffn_w_up, l1_ffn_conv_w, l1_ffn_conv_b, l1_ffn_w_down, l1_ln2_g, l1_ln2_b)
    args = (p0, na_tiles, ln0, f0, p1, l1_sinks.astype(F32), ln1, f1)
    return (_encoder(x_prompt, *args), _encoder(x_sample, *args))
```

```python
import functools

import numpy as np
import jax
import jax.numpy as jnp
from jax import lax
from jax.experimental import pallas as pl
from jax.experimental.pallas import tpu as pltpu

F32 = jnp.float32
BF16 = jnp.bfloat16

D_MODEL = 1024
DEPTH = 2
GRID_W = 64
HEAD_DIM = 64
NA_HEADS = 8
NA_KH = 8
NA_KW = 16
MLA_HEADS = 8
MLA_Q_RANK = 384
MLA_KV_RANK = 256
MLA_NOPE = 64
MLA_ROPE = 32
MLA_V = 64
MLA_THETA = 10000.0
GQA_Q_HEADS = 16
GQA_KV_HEADS = 4
WINDOW = 128
ROPE_THETA = 500000.0
ROT_DIM = HEAD_DIM // 4
D_FF = 2816
ALPHA = (2 * DEPTH) ** 0.25
NORM_EPS = 1e-5
NA_WIDTH = NA_HEADS * HEAD_DIM

LANES = 128
BF16_ROWS = 16
NEG = -1e30
VMEM_LIMIT = 56 * 1024 * 1024

PROJ_TM = 512
OUT_TM = PROJ_TM
FFN_TM = 1024
FFN_CK = 256
FFN_HALO = BF16_ROWS
MLA_TQ = 256
MLA_NSUB = 4
MLA_UNROLL = 1
MLA_VROWS = MLA_V + BF16_ROWS
NA_ROWS = 8
NA_WIN = 16
WIN_TQ = 256
WIN_VROWS = HEAD_DIM + BF16_ROWS

_NT = (((1,), (1,)), ((), ()))
_TN = (((0,), (0,)), ((), ()))


def _params(sem):
    return pltpu.CompilerParams(dimension_semantics=sem, vmem_limit_bytes=VMEM_LIMIT)


def _full(shape):
    n = len(shape)
    return pl.BlockSpec(shape, lambda *_: (0,) * n)


def _layer_norm(z, g, b):
    mu = jnp.mean(z, axis=-1, keepdims=True)
    zc = z - mu
    var = jnp.mean(zc * zc, axis=-1, keepdims=True)
    return zc * lax.rsqrt(var + NORM_EPS) * g + b


def _rms_norm(x, g):
    return x * lax.rsqrt(jnp.mean(x * x, axis=-1, keepdims=True) + NORM_EPS) * g


def _l0_proj_kernel(x_ref, wn_ref, wkat_ref, gq_ref, gkv_ref, wqupt_ref, wkn_ref, wvt_ref,
                    cq_ref, sq_ref, cqt_ref, sqt_ref,
                    qa_ref, kat_ref, va_ref, qt_ref, kc_ref, vt_ref):
    xb = x_ref[0].astype(BF16)
    hn = jnp.dot(xb, wn_ref[...], preferred_element_type=F32)
    tm = xb.shape[0]
    qa_ref[0] = (hn[:, 0:NA_WIDTH] * (HEAD_DIM ** -0.5)).astype(BF16)
    va_ref[0] = hn[:, NA_WIDTH:2 * NA_WIDTH].astype(BF16)
    kat_ref[0] = lax.dot_general(wkat_ref[...], xb, _NT, preferred_element_type=F32).astype(BF16)
    o = 2 * NA_WIDTH
    qn = _rms_norm(hn[:, o:o + MLA_Q_RANK], gq_ref[...]).astype(BF16)
    o += MLA_Q_RANK
    kvn = _rms_norm(hn[:, o:o + MLA_KV_RANK], gkv_ref[...]).astype(BF16)
    o += MLA_KV_RANK
    kr = hn[:, o:o + LANES] * cq_ref[...] + hn[:, o + LANES:o + 2 * LANES] * sq_ref[...]
    hw = MLA_HEADS * LANES
    qqt = lax.dot_general(wqupt_ref[...], qn, _NT, preferred_element_type=F32)
    cqt = cqt_ref[...]
    sqt = sqt_ref[...]
    kn = jnp.dot(kvn, wkn_ref[...], preferred_element_type=F32)
    vt = lax.dot_general(wvt_ref[...], kvn, _NT, preferred_element_type=F32)
    ones_rows = jnp.where(lax.broadcasted_iota(jnp.int32, (BF16_ROWS, tm), 0) == 0, 1.0, 0.0).astype(BF16)
    for h in range(MLA_HEADS):
        a = qqt[h * LANES:(h + 1) * LANES]
        b = qqt[hw + h * LANES:hw + (h + 1) * LANES]
        qt_ref[0, h, 0] = (a * cqt + b * sqt).astype(BF16)
        kc_ref[0, :, h * LANES:(h + 1) * LANES] = (kn[:, h * LANES:(h + 1) * LANES] + kr).astype(BF16)
        vt_ref[0, h, 0, 0:MLA_V, :] = vt[h * MLA_V:(h + 1) * MLA_V].astype(BF16)
        vt_ref[0, h, 0, MLA_V:MLA_VROWS, :] = ones_rows


def _l0_proj(x, w, tabs):
    B, S, D = x.shape
    tm = PROJ_TM
    nt = S // tm
    hw = MLA_HEADS * LANES
    cq, sq, cqt, sqt = tabs
    row = lambda b, i: (b, i, 0)
    chunk = lambda b, i: (b, 0, i, 0, 0)
    out_shape = (
        jax.ShapeDtypeStruct((B, S, NA_WIDTH), BF16),
        jax.ShapeDtypeStruct((B, NA_WIDTH, S), BF16),
        jax.ShapeDtypeStruct((B, S, NA_WIDTH), BF16),
        jax.ShapeDtypeStruct((B, MLA_HEADS, nt, LANES, tm), BF16),
        jax.ShapeDtypeStruct((B, S, hw), BF16),
        jax.ShapeDtypeStruct((B, MLA_HEADS, nt, MLA_VROWS, tm), BF16),
    )
    return pl.pallas_call(
        _l0_proj_kernel,
        grid=(B, nt),
        in_specs=[
            pl.BlockSpec((1, tm, D), row),
            _full(w['wn'].shape), _full(w['wkat'].shape), _full(w['gq'].shape), _full(w['gkv'].shape),
            _full(w['wqupt'].shape), _full(w['wkn'].shape), _full(w['wvt'].shape),
            pl.BlockSpec((tm, LANES), lambda b, i: (i, 0)),
            pl.BlockSpec((tm, LANES), lambda b, i: (i, 0)),
            pl.BlockSpec((LANES, tm), lambda b, i: (0, i)),
            pl.BlockSpec((LANES, tm), lambda b, i: (0, i)),
        ],
        out_specs=(
            pl.BlockSpec((1, tm, NA_WIDTH), row),
            pl.BlockSpec((1, NA_WIDTH, tm), lambda b, i: (b, 0, i)),
            pl.BlockSpec((1, tm, NA_WIDTH), row),
            pl.BlockSpec((1, MLA_HEADS, 1, LANES, tm), chunk),
            pl.BlockSpec((1, tm, hw), row),
            pl.BlockSpec((1, MLA_HEADS, 1, MLA_VROWS, tm), chunk),
        ),
        out_shape=out_shape,
        compiler_params=_params(("parallel", "parallel")),
        name="l0_proj",
    )(x, w['wn'], w['wkat'], w['gq'], w['gkv'], w['wqupt'], w['wkn'], w['wvt'], cq, sq, cqt, sqt)


def _na_kernel(q_ref, k0_ref, k1_ref, k2_ref, k3_ref, v0_ref, v1_ref, v2_ref, v3_ref, y2_ref, o_ref, tab_scr):
    r = pl.program_id(2)
    nblk = pl.num_programs(2)
    case = jnp.where(r == 0, 0, jnp.where(r == nblk - 1, 2, 1))

    @pl.when((r <= 1) | (r == nblk - 1))
    def _():
        left = lax.broadcasted_iota(jnp.int32, (GRID_W, LANES), 1) < GRID_W
        for i in range(NA_ROWS):
            first_key = jnp.where(case == 0, max(i - NA_KH // 2, 0),
                                  jnp.where(case == 1, i, min(i + NA_KH // 2, NA_ROWS)))
            for jp in range(NA_WIN // 2):
                ok_l = (2 * jp >= first_key) & (2 * jp < first_key + NA_KH)
                ok_r = (2 * jp + 1 >= first_key) & (2 * jp + 1 < first_key + NA_KH)
                pen = jnp.where(left, jnp.where(ok_l, 0.0, NEG), jnp.where(ok_r, 0.0, NEG)).astype(F32)
                a = jnp.clip(2 * jp - i + NA_KH - (NA_KH // 2) * case, 0, 2 * NA_KH - 1)
                for hh in range(2):
                    tab_scr[hh, i * GRID_W:(i + 1) * GRID_W, jp * LANES:(jp + 1) * LANES] = y2_ref[hh, a] + pen

    q2 = q_ref[0]
    ks = (k0_ref[0], k1_ref[0], k2_ref[0], k3_ref[0])
    vs = (v0_ref[0], v1_ref[0], v2_ref[0], v3_ref[0])
    nq = q2.shape[0]
    tkp = ks[0].shape[1]
    upper = lax.broadcasted_iota(jnp.int32, (nq, LANES), 1) >= HEAD_DIM
    scores = []
    for hh in range(2):
        qh = jnp.where(upper == (hh == 1), q2, jnp.zeros_like(q2))
        scores.append([jnp.dot(qh, ks[t], preferred_element_type=F32) + tab_scr[hh, :, t * tkp:(t + 1) * tkp]
                       for t in range(4)])
    probs = []
    for s in scores:
        m = functools.reduce(jnp.maximum, [jnp.max(st, axis=1, keepdims=True) for st in s])
        p = [jnp.exp(st - m) for st in s]
        l = functools.reduce(jnp.add, [jnp.sum(pt, axis=1, keepdims=True) for pt in p])
        probs.append(([pt.astype(BF16) for pt in p], l))
    outs = [functools.reduce(jnp.add, [jnp.dot(p[t], vs[t], preferred_element_type=F32) for t in range(4)]) / l
            for p, l in probs]
    o_ref[0] = jnp.where(upper, outs[1], outs[0]).astype(BF16)


def _na_attention(qa, kat, va, y2):
    B, S, _ = qa.shape
    rows = S // GRID_W
    nblk = rows // NA_ROWS
    assert nblk >= 2 and rows % NA_ROWS == 0
    tq = NA_ROWS * GRID_W
    tkp = (NA_WIN // 4) * GRID_W
    npieces = S // tkp
    hp_n = NA_HEADS // 2

    def wstart(r):
        return jnp.clip(2 * r - 1, 0, npieces - 4)

    k_specs = [pl.BlockSpec((1, LANES, tkp), lambda hp, b, r, t=t: (b, hp, wstart(r) + t)) for t in range(4)]
    v_specs = [pl.BlockSpec((1, tkp, LANES), lambda hp, b, r, t=t: (b, wstart(r) + t, hp)) for t in range(4)]
    return pl.pallas_call(
        _na_kernel,
        grid=(hp_n, B, nblk),
        in_specs=[pl.BlockSpec((1, tq, LANES), lambda hp, b, r: (b, r, hp))] + k_specs + v_specs + [
            pl.BlockSpec((2, 2 * NA_KH, GRID_W, LANES), lambda hp, b, r: (hp, 0, 0, 0))],
        out_specs=pl.BlockSpec((1, tq, LANES), lambda hp, b, r: (b, r, hp)),
        out_shape=jax.ShapeDtypeStruct((B, S, NA_WIDTH), BF16),
        scratch_shapes=[pltpu.VMEM((2, tq, 4 * tkp), F32)],
        compiler_params=_params(("arbitrary", "arbitrary", "arbitrary")),
        name="na_attn",
    )(qa, kat, kat, kat, kat, va, va, va, va, y2)


def _na_bias_tiles(rpb):
    cq = np.arange(GRID_W)[:, None]
    kc = np.arange(GRID_W)[None, :]
    cs = np.clip(cq - NA_KW // 2, 0, GRID_W - NA_KW)
    col_ok = (kc >= cs) & (kc < cs + NA_KW)
    onehot = ((kc - cq + NA_KW - 1)[None] == np.arange(2 * NA_KW - 1)[:, None, None]) & col_ok[None]
    y = jnp.einsum('hab,bqk->haqk', rpb.astype(F32), jnp.asarray(onehot, F32), precision=lax.Precision.HIGHEST)
    y = jnp.where(col_ok[None, None], y, NEG)
    y = jnp.pad(y, ((0, 0), (1, 1), (0, 0), (0, 0)), constant_values=NEG)
    return jnp.concatenate([y[:, :-1], y[:, 1:]], axis=-1)


def _mla_kernel(q_ref, k_ref, v_ref, o_ref, s_scr, *, nq, nk, tk, tq, nsub, unroll, c):
    per_chunk = tk // tq

    def q_body(qi, carry):
        where = [(qi * (nsub // per_chunk) + j // per_chunk, (j % per_chunk) * tq) for j in range(nsub)]
        qts = [q_ref[0, 0, ci, :, lo:lo + tq] for ci, lo in where]

        def scores(kc, slot):
            k = k_ref[0, pl.ds(pl.multiple_of(kc * tk, tk), tk), :]
            for j, qt in enumerate(qts):
                s_scr[slot, j] = jnp.dot(k, qt, preferred_element_type=F32)

        def fold(kc, slot, mc):
            v = v_ref[0, 0, kc]
            stats = []
            for j, (m_old, _) in enumerate(mc):
                st = s_scr[slot, j]
                m_new = jnp.maximum(m_old, jnp.max(st, axis=0, keepdims=True))
                alpha = jnp.exp2((m_old - m_new) * c)
                pt = jnp.exp2((st - m_new) * c).astype(BF16)
                stats.append((m_new, alpha, pt))
            return tuple((m_new, acc * alpha + jnp.dot(v, pt, preferred_element_type=F32))
                         for (_, acc), (m_new, alpha, pt) in zip(mc, stats))

        def k_body(i, mc):
            scores(2 * i + 1, 1)
            mc = fold(2 * i, 0, mc)
            scores(2 * i + 2, 0)
            return fold(2 * i + 1, 1, mc)

        init = tuple((jnp.full((1, tq), -jnp.inf, F32), jnp.zeros((MLA_VROWS, tq), F32)) for _ in range(nsub))
        scores(0, 0)
        mc = lax.fori_loop(0, nk // 2 - 1, k_body, init, unroll=unroll)
        scores(nk - 1, 1)
        res = fold(nk - 1, 1, fold(nk - 2, 0, mc))
        for (_, acc), (ci, lo) in zip(res, where):
            o_ref[0, 0, ci, :, lo:lo + tq] = (acc[0:MLA_V] / acc[MLA_V:MLA_V + 1]).astype(BF16)
        return carry

    lax.fori_loop(0, nq, q_body, 0)


def _mla_attention(qt, kc, vt):
    B, H, nt, _, tm = qt.shape
    S = nt * tm
    c = float((MLA_NOPE + MLA_ROPE) ** -0.5 * np.log2(np.e))
    head = lambda b, h: (b, h, 0, 0, 0)
    nsub = MLA_NSUB
    assert nsub % (tm // MLA_TQ) == 0 and (nt * (tm // MLA_TQ)) % nsub == 0 and nt % 2 == 0
    return pl.pallas_call(
        functools.partial(_mla_kernel, nq=nt * (tm // MLA_TQ) // nsub, nk=nt, tk=tm, tq=MLA_TQ, nsub=nsub,
                          unroll=MLA_UNROLL, c=c),
        grid=(B, H),
        in_specs=[
            pl.BlockSpec((1, 1, nt, LANES, tm), head),
            pl.BlockSpec((1, S, LANES), lambda b, h: (b, 0, h)),
            pl.BlockSpec((1, 1, nt, MLA_VROWS, tm), head),
        ],
        out_specs=pl.BlockSpec((1, 1, nt, MLA_V, tm), head),
        out_shape=jax.ShapeDtypeStruct((B, H, nt, MLA_V, tm), BF16),
        scratch_shapes=[pltpu.VMEM((2, nsub, tm, MLA_TQ), F32)],
        compiler_params=_params(("parallel", "parallel")),
        name="mla_attn",
    )(qt, kc, vt)


def _out_proj_kernel(*refs, n_act, has_t):
    x_ref = refs[0]
    a_refs = refs[1:1 + n_act]
    w_refs = refs[1 + n_act:1 + 2 * n_act]
    g_ref, b_ref, o_ref = refs[1 + 2 * n_act:]
    mix = None
    for k, (a, w) in enumerate(zip(a_refs, w_refs)):
        if has_t and k == n_act - 1:
            at = a[0, :, 0]
            at = at.reshape(at.shape[0] * at.shape[1], at.shape[2])
            t = lax.dot_general(at, w[...], _TN, preferred_element_type=F32)
        else:
            t = jnp.dot(a[...], w[...], preferred_element_type=F32)
        mix = t if mix is None else mix + t
    o_ref[...] = _layer_norm(ALPHA * x_ref[...] + mix, g_ref[...], b_ref[...])


def _out_proj(x2, acts, ws, g, b, act_t=None, w_t=None):
    N, D = x2.shape
    tm = OUT_TM
    row = lambda i: (i, 0)
    specs = [pl.BlockSpec((tm, a.shape[1]), row) for a in acts]
    acts, ws = list(acts), list(ws)
    if act_t is not None:
        _, H, nt, hd, tmc = act_t.shape
        assert tmc == tm
        specs.append(pl.BlockSpec((1, H, 1, hd, tm), lambda i: (i // nt, 0, i % nt, 0, 0)))
        acts.append(act_t)
        ws.append(w_t)
    return pl.pallas_call(
        functools.partial(_out_proj_kernel, n_act=len(acts), has_t=act_t is not None),
        grid=(N // tm,),
        in_specs=[pl.BlockSpec((tm, D), row)] + specs + [_full(w.shape) for w in ws] + [_full(g.shape), _full(b.shape)],
        out_specs=pl.BlockSpec((tm, D), row),
        out_shape=jax.ShapeDtypeStruct((N, D), F32),
        compiler_params=_params(("parallel",)),
        name="out_proj",
    )(x2, *acts, *ws, g, b)


def _ffn_kernel(x_ref, xp_ref, xn_ref, wup_ref, cw_ref, cb_ref, wdn_ref, g_ref, b_ref, o_ref, xb_scr, act_scr):
    i = pl.program_id(1)
    tm = x_ref.shape[1]
    hl = FFN_HALO
    ck = FFN_CK
    xp = jnp.where(i > 0, xp_ref[0], jnp.zeros_like(xp_ref[0]))
    xn = jnp.where(i < pl.num_programs(1) - 1, xn_ref[0], jnp.zeros_like(xn_ref[0]))
    xb_scr[0:hl, :] = xp.astype(BF16)
    xb_scr[hl:hl + tm, :] = x_ref[0].astype(BF16)
    xb_scr[hl + tm:hl + tm + hl, :] = xn.astype(BF16)

    def conv(c0):
        h = jnp.dot(xb_scr[...], wup_ref[:, c0:c0 + ck], preferred_element_type=F32)
        cw = cw_ref[:, c0:c0 + ck]
        return (h[hl - 1:hl - 1 + tm] * cw[0:1] + h[hl:hl + tm] * cw[1:2] + h[hl + 1:hl + 1 + tm] * cw[2:3]
                + cb_ref[:, c0:c0 + ck])

    for c in range(D_FF // ck):
        gate = conv(c * ck)
        val = conv(D_FF + c * ck)
        act = 0.5 * gate * (1.0 + lax.erf(gate * (2.0 ** -0.5))) * val
        act_scr[:, c * ck:(c + 1) * ck] = act.astype(BF16)
    y = jnp.dot(act_scr[...], wdn_ref[...], preferred_element_type=F32)
    o_ref[0] = _layer_norm(ALPHA * x_ref[0] + y, g_ref[...], b_ref[...])


def _ffn(x, w):
    B, S, D = x.shape
    tm = min(FFN_TM, S)
    hl = FFN_HALO
    nh = S // hl
    r = tm // hl
    once = dict(pipeline_mode=pl.Buffered(1))
    return pl.pallas_call(
        _ffn_kernel,
        grid=(B, S // tm),
        in_specs=[
            pl.BlockSpec((1, tm, D), lambda b, i: (b, i, 0)),
            pl.BlockSpec((1, hl, D), lambda b, i: (b, jnp.maximum(i * r - 1, 0), 0)),
            pl.BlockSpec((1, hl, D), lambda b, i: (b, jnp.minimum((i + 1) * r, nh - 1), 0)),
            pl.BlockSpec(w['wup'].shape, lambda b, i: (0, 0), **once),
            _full(w['cw'].shape), _full(w['cb'].shape),
            pl.BlockSpec(w['wdn'].shape, lambda b, i: (0, 0), **once),
            _full(w['g'].shape), _full(w['b'].shape),
        ],
        out_specs=pl.BlockSpec((1, tm, D), lambda b, i: (b, i, 0)),
        out_shape=jax.ShapeDtypeStruct((B, S, D), F32),
        scratch_shapes=[pltpu.VMEM((tm + 2 * hl, D), BF16), pltpu.VMEM((tm, D_FF), BF16)],
        compiler_params=_params(("parallel", "parallel")),
        name="ffn",
    )(x, x, x, w['wup'], w['cw'], w['cb'], w['wdn'], w['g'], w['b'])


def _l1_proj_kernel(x_ref, wn_ref, wt_ref, ck_ref, sk_ref, cqt_ref, sqt_ref, qt_ref, k_ref, vt_ref):
    xb = x_ref[0].astype(BF16)
    tm = xb.shape[0]
    hn = jnp.dot(xb, wn_ref[...], preferred_element_type=F32)
    ht = lax.dot_general(wt_ref[...], xb, _NT, preferred_element_type=F32)
    qw = GQA_Q_HEADS * HEAD_DIM
    kw2 = GQA_KV_HEADS * LANES
    ck = ck_ref[...]
    sk = sk_ref[...]
    for g in range(GQA_KV_HEADS):
        k_ref[0, :, g * LANES:(g + 1) * LANES] = (
            hn[:, g * LANES:(g + 1) * LANES] * ck + hn[:, kw2 + g * LANES:kw2 + (g + 1) * LANES] * sk).astype(BF16)
    cqt = cqt_ref[...]
    sqt = sqt_ref[...]
    for p in range(qw // LANES):
        a = ht[p * LANES:(p + 1) * LANES]
        b = ht[qw + p * LANES:qw + (p + 1) * LANES]
        qt_ref[0, p] = ((a * cqt + b * sqt) * (HEAD_DIM ** -0.5)).astype(BF16)
    ones_rows = jnp.where(lax.broadcasted_iota(jnp.int32, (BF16_ROWS, tm), 0) == 0, 1.0, 0.0).astype(BF16)
    for g in range(GQA_KV_HEADS):
        vt_ref[0, g, 0:HEAD_DIM, :] = ht[2 * qw + g * HEAD_DIM:2 * qw + (g + 1) * HEAD_DIM].astype(BF16)
        vt_ref[0, g, HEAD_DIM:WIN_VROWS, :] = ones_rows


def _l1_proj(x, w, tabs):
    B, S, D = x.shape
    tm = PROJ_TM
    ck, sk, cqt, sqt = tabs
    npair = GQA_Q_HEADS // 2
    kw2 = GQA_KV_HEADS * LANES
    row = lambda b, i: (b, i, 0)
    return pl.pallas_call(
        _l1_proj_kernel,
        grid=(B, S // tm),
        in_specs=[
            pl.BlockSpec((1, tm, D), row), _full(w['wn'].shape), _full(w['wt'].shape),
            pl.BlockSpec((tm, LANES), lambda b, i: (i, 0)),
            pl.BlockSpec((tm, LANES), lambda b, i: (i, 0)),
            pl.BlockSpec((LANES, tm), lambda b, i: (0, i)),
            pl.BlockSpec((LANES, tm), lambda b, i: (0, i)),
        ],
        out_specs=(
            pl.BlockSpec((1, npair, LANES, tm), lambda b, i: (b, 0, 0, i)),
            pl.BlockSpec((1, tm, kw2), row),
            pl.BlockSpec((1, GQA_KV_HEADS, WIN_VROWS, tm), lambda b, i: (b, 0, 0, i)),
        ),
        out_shape=(
            jax.ShapeDtypeStruct((B, npair, LANES, S), BF16),
            jax.ShapeDtypeStruct((B, S, kw2), BF16),
            jax.ShapeDtypeStruct((B, GQA_KV_HEADS, WIN_VROWS, S), BF16),
        ),
        compiler_params=_params(("parallel", "parallel")),
        name="l1_proj",
    )(x, w['wn'], w['wt'], ck, sk, cqt, sqt)


def _win_kernel(sink_ref, q_ref, kp_ref, km_ref, kn_ref, vp_ref, vm_ref, vn_ref, o_ref, kw_scr, vw_scr):
    i = pl.program_id(1)
    last = pl.num_programs(1) - 1
    tq = q_ref.shape[3]
    hb = kp_ref.shape[1]
    tk = tq + 2 * hb
    kw_scr[0:hb, :] = kp_ref[0]
    kw_scr[hb:hb + tq, :] = km_ref[0]
    kw_scr[hb + tq:tk, :] = kn_ref[0]
    vw_scr[:, :, 0:hb] = vp_ref[0]
    vw_scr[:, :, hb:hb + tq] = vm_ref[0]
    vw_scr[:, :, hb + tq:tk] = vn_ref[0]
    kr = lax.broadcasted_iota(jnp.int32, (tk, tq), 0)
    qc = lax.broadcasted_iota(jnp.int32, (tk, tq), 1)
    ok = (jnp.abs(kr - hb - qc) <= WINDOW) & ((kr >= hb) | (i > 0)) & ((kr < hb + tq) | (i < last))
    mask = jnp.where(ok, 0.0, NEG).astype(F32)
    upper = lax.broadcasted_iota(jnp.int32, (LANES, tq), 0) >= HEAD_DIM
    group = GQA_Q_HEADS // GQA_KV_HEADS
    for g in range(GQA_KV_HEADS):
        k = kw_scr[:, g * LANES:(g + 1) * LANES]
        v = vw_scr[g]
        scores = []
        for j in range(group):
            q2 = q_ref[0, (g * group + j) // 2]
            qh = jnp.where(upper == (j % 2 == 1), q2, jnp.zeros_like(q2))
            scores.append(jnp.dot(k, qh, preferred_element_type=F32) + mask)
        probs = []
        for j, st in enumerate(scores):
            sink = sink_ref[g * group + j]
            m = jnp.maximum(jnp.max(st, axis=0, keepdims=True), sink)
            probs.append((jnp.exp(st - m).astype(BF16), jnp.exp(sink - m)))
        for j, (pt, esink) in enumerate(probs):
            acc = jnp.dot(v, pt, preferred_element_type=F32)
            o = acc[0:HEAD_DIM] / (acc[HEAD_DIM:HEAD_DIM + 1] + esink)
            o_ref[0, g * group + j, 0] = o.astype(BF16)


def _win_attention(qt, k, vt, sinks):
    B, npair, _, S = qt.shape
    tq = min(WIN_TQ, S)
    hb = WINDOW
    r = tq // hb
    nhb = S // hb
    kw2 = k.shape[2]
    tm = OUT_TM
    per = tm // tq
    prev = lambda i: jnp.maximum(i * r - 1, 0)
    nxt = lambda i: jnp.minimum((i + 1) * r, nhb - 1)
    return pl.pallas_call(
        _win_kernel,
        grid_spec=pltpu.PrefetchScalarGridSpec(
            num_scalar_prefetch=1,
            grid=(B, S // tq),
            in_specs=[
                pl.BlockSpec((1, npair, LANES, tq), lambda b, i, s: (b, 0, 0, i)),
                pl.BlockSpec((1, hb, kw2), lambda b, i, s: (b, prev(i), 0)),
                pl.BlockSpec((1, tq, kw2), lambda b, i, s: (b, i, 0)),
                pl.BlockSpec((1, hb, kw2), lambda b, i, s: (b, nxt(i), 0)),
                pl.BlockSpec((1, GQA_KV_HEADS, WIN_VROWS, hb), lambda b, i, s: (b, 0, 0, prev(i))),
                pl.BlockSpec((1, GQA_KV_HEADS, WIN_VROWS, tq), lambda b, i, s: (b, 0, 0, i)),
                pl.BlockSpec((1, GQA_KV_HEADS, WIN_VROWS, hb), lambda b, i, s: (b, 0, 0, nxt(i))),
            ],
            out_specs=pl.BlockSpec((1, GQA_Q_HEADS, 1, HEAD_DIM, tq), lambda b, i, s: (b, 0, i // per, 0, i % per)),
            scratch_shapes=[pltpu.VMEM((tq + 2 * hb, kw2), BF16),
                            pltpu.VMEM((GQA_KV_HEADS, WIN_VROWS, tq + 2 * hb), BF16)],
        ),
        out_shape=jax.ShapeDtypeStruct((B, GQA_Q_HEADS, S // tm, HEAD_DIM, tm), BF16),
        compiler_params=_params(("parallel", "parallel")),
        name="win_attn",
    )(sinks, qt, k, k, k, vt, vt, vt)


def _rope_tables(S, theta, rot_dim):
    inv = 1.0 / (theta ** (jnp.arange(0, rot_dim, 2, dtype=F32) / rot_dim))
    ang = jnp.arange(S, dtype=F32)[:, None] * inv[None, :]
    cos, sin = jnp.cos(ang), jnp.sin(ang)
    return jnp.concatenate([cos, cos], axis=1), jnp.concatenate([-sin, sin], axis=1)


def _swap_halves_cols(w, rot_dim):
    half = rot_dim // 2
    return jnp.concatenate([w[:, half:rot_dim], w[:, :half]], axis=1)


def _prep_l0(w_in, g_q, w_q_up, g_kv, w_kv_up, w_out):
    d = w_in.shape[0]
    s = np.cumsum([0, NA_WIDTH, NA_WIDTH, NA_WIDTH, MLA_Q_RANK, MLA_KV_RANK, MLA_ROPE])
    w_qa, w_ka, w_va, w_ql, w_kvl, w_kr = [w_in[:, s[k]:s[k + 1]] for k in range(6)]
    H = MLA_HEADS
    qd = MLA_NOPE + MLA_ROPE

    def rope_block(w):
        return jnp.concatenate([jnp.zeros((d, MLA_NOPE), F32), w, jnp.zeros((d, LANES - qd), F32)], axis=1)

    wn = jnp.concatenate([w_qa, w_va, w_ql, w_kvl, rope_block(w_kr), rope_block(_swap_halves_cols(w_kr, MLA_ROPE))],
                         axis=1).astype(BF16)
    wq = w_q_up.reshape(MLA_Q_RANK, H, qd)
    pad = jnp.zeros((MLA_Q_RANK, H, LANES - qd), F32)
    wq_main = jnp.concatenate([wq, pad], axis=2)
    wq_sw = jnp.concatenate([jnp.zeros((MLA_Q_RANK, H, MLA_NOPE), F32),
                             wq[:, :, MLA_NOPE + MLA_ROPE // 2:], wq[:, :, MLA_NOPE:MLA_NOPE + MLA_ROPE // 2],
                             pad], axis=2)
    wqupt = jnp.concatenate([wq_main.reshape(MLA_Q_RANK, H * LANES), wq_sw.reshape(MLA_Q_RANK, H * LANES)],
                            axis=1).T.astype(BF16)
    wkv = w_kv_up.reshape(MLA_KV_RANK, H, MLA_NOPE + MLA_V)
    wkn = jnp.concatenate([wkv[:, :, :MLA_NOPE], jnp.zeros((MLA_KV_RANK, H, LANES - MLA_NOPE), F32)],
                          axis=2).reshape(MLA_KV_RANK, H * LANES).astype(BF16)
    wvt = wkv[:, :, MLA_NOPE:].reshape(MLA_KV_RANK, H * MLA_V).T.astype(BF16)
    return dict(wn=wn, wkat=w_ka.T.astype(BF16), gq=g_q.reshape(1, -1), gkv=g_kv.reshape(1, -1), wqupt=wqupt,
                wkn=wkn, wvt=wvt, woa=w_out[:NA_WIDTH].astype(BF16), wob=w_out[NA_WIDTH:].astype(BF16))


def _l0_tables(S):
    cosf, sins = _rope_tables(S, MLA_THETA, MLA_ROPE)
    ones = jnp.ones((S, MLA_NOPE), F32)
    zc = jnp.zeros((S, LANES - MLA_NOPE - MLA_ROPE), F32)
    cq = jnp.concatenate([ones, cosf, zc], axis=1)
    sq = jnp.concatenate([jnp.zeros((S, MLA_NOPE), F32), sins, zc], axis=1)
    return cq, sq, cq.T, sq.T


def _prep_l1(w_in, w_out):
    qw = GQA_Q_HEADS * HEAD_DIM
    kw = GQA_KV_HEADS * HEAD_DIM
    wq, wk, wv = w_in[:, :qw], w_in[:, qw:qw + kw], w_in[:, qw + kw:]
    d = w_in.shape[0]

    def partner(w, heads):
        wh = w.reshape(d, heads, HEAD_DIM)
        half = ROT_DIM // 2
        return jnp.concatenate([wh[:, :, half:ROT_DIM], wh[:, :, :half],
                                jnp.zeros((d, heads, HEAD_DIM - ROT_DIM), F32)], axis=2).reshape(d, heads * HEAD_DIM)

    def dup(w):
        return jnp.concatenate([w.reshape(d, GQA_KV_HEADS, HEAD_DIM)] * 2, axis=2).reshape(d, GQA_KV_HEADS * LANES)

    wn = jnp.concatenate([dup(wk), dup(partner(wk, GQA_KV_HEADS))], axis=1).astype(BF16)
    wt = jnp.concatenate([wq, partner(wq, GQA_Q_HEADS), wv], axis=1).T.astype(BF16)
    return dict(wn=wn, wt=wt, wo=w_out.astype(BF16))


def _l1_tables(S):
    cosf, sins = _rope_tables(S, ROPE_THETA, ROT_DIM)
    c64 = jnp.concatenate([cosf, jnp.ones((S, HEAD_DIM - ROT_DIM), F32)], axis=1)
    s64 = jnp.concatenate([sins, jnp.zeros((S, HEAD_DIM - ROT_DIM), F32)], axis=1)
    c128, s128 = jnp.concatenate([c64, c64], axis=1), jnp.concatenate([s64, s64], axis=1)
    return c128, s128, c128.T, s128.T


def _prep_ffn(w_up, conv_w, conv_b, w_down, g, b):
    return dict(wup=w_up.astype(BF16), cw=conv_w, cb=conv_b.reshape(1, -1),
                wdn=w_down.astype(BF16), g=g.reshape(1, -1), b=b.reshape(1, -1))


def _encoder(x, p0, na_tiles, ln0, f0, p1, sinks, ln1, f1):
    B, S, D = x.shape
    qa, kat, va, qt, kc, vt = _l0_proj(x, p0, _l0_tables(S))
    a_out = _na_attention(qa, kat, va, na_tiles)
    b_out_t = _mla_attention(qt, kc, vt)
    x = _out_proj(x.reshape(B * S, D), [a_out.reshape(B * S, -1)], [p0['woa']], *ln0,
                  act_t=b_out_t, w_t=p0['wob']).reshape(B, S, D)
    x = _ffn(x, f0)
    qt1, k1, vt1 = _l1_proj(x, p1, _l1_tables(S))
    attn_t = _win_attention(qt1, k1, vt1, sinks)
    x = _out_proj(x.reshape(B * S, D), [], [], *ln1, act_t=attn_t, w_t=p1['wo']).reshape(B, S, D)
    return _ffn(x, f1)


def kernel(x_prompt, x_sample, l0_w_in, l0_rpb, l0_g_q_norm, l0_w_q_up, l0_g_kv_norm, l0_w_kv_up, l0_w_out, l0_ln1_g, l0_ln1_b, l0_ffn_w_up, l0_ffn_conv_w, l0_ffn_conv_b, l0_ffn_w_down, l0_ln2_g, l0_ln2_b, l1_w_in, l1_sinks, l1_w_out, l1_ln1_g, l1_ln1_b, l1_ffn_w_up, l1_ffn_conv_w, l1_ffn_conv_b, l1_ffn_w_down, l1_ln2_g, l1_ln2_b):
    p0 = _prep_l0(l0_w_in, l0_g_q_norm, l0_w_q_up, l0_g_kv_norm, l0_w_kv_up, l0_w_out)
    na_tiles = _na_bias_tiles(l0_rpb)
    ln0 = (l0_ln1_g.reshape(1, -1), l0_ln1_b.reshape(1, -1))
    f0 = _prep_ffn(l0_ffn_w_up, l0_ffn_conv_w, l0_ffn_conv_b, l0_ffn_w_down, l0_ln2_g, l0_ln2_b)
    p1 = _prep_l1(l1_w_in, l1_w_out)
    ln1 = (l1_ln1_g.reshape(1, -1), l1_ln1_b.reshape(1, -1))
    f1 = _prep_ffn(l1_ffn_w_up, l1_ffn_conv_w, l1_ffn_conv_b, l1_ffn_w_down, l1_ln2_g, l1_ln2_b)
    args = (p0, na_tiles, ln0, f0, p1, l1_sinks.astype(F32), ln1, f1)
    return (_encoder(x_prompt, *args), _encoder(x_sample, *args))
```

```python
import functools

import numpy as np
import jax
import jax.numpy as jnp
from jax import lax
from jax.experimental import pallas as pl
from jax.experimental.pallas import tpu as pltpu

F32 = jnp.float32
BF16 = jnp.bfloat16

D_MODEL = 1024
DEPTH = 2
GRID_W = 64
HEAD_DIM = 64
NA_HEADS = 8
NA_KH = 8
NA_KW = 16
MLA_HEADS = 8
MLA_Q_RANK = 384
MLA_KV_RANK = 256
MLA_NOPE = 64
MLA_ROPE = 32
MLA_V = 64
MLA_THETA = 10000.0
GQA_Q_HEADS = 16
GQA_KV_HEADS = 4
WINDOW = 128
ROPE_THETA = 500000.0
ROT_DIM = HEAD_DIM // 4
D_FF = 2816
ALPHA = (2 * DEPTH) ** 0.25
NORM_EPS = 1e-5
NA_WIDTH = NA_HEADS * HEAD_DIM

LANES = 128
BF16_ROWS = 16
NEG = -1e30
VMEM_LIMIT = 56 * 1024 * 1024

PROJ_TM = 512
OUT_TM = PROJ_TM
FFN_TM = 1024
FFN_CK = 256
FFN_HALO = BF16_ROWS
MLA_TQ = 256
MLA_NSUB = 4
MLA_UNROLL = 2
MLA_VROWS = MLA_V + BF16_ROWS
NA_ROWS = 8
NA_WIN = 16
NA_VROWS = HEAD_DIM + BF16_ROWS
WIN_TQ = 256
WIN_VROWS = HEAD_DIM + BF16_ROWS

_NT = (((1,), (1,)), ((), ()))
_TN = (((0,), (0,)), ((), ()))


def _params(sem):
    return pltpu.CompilerParams(dimension_semantics=sem, vmem_limit_bytes=VMEM_LIMIT)


def _full(shape):
    n = len(shape)
    return pl.BlockSpec(shape, lambda *_: (0,) * n)


def _layer_norm(z, g, b):
    mu = jnp.mean(z, axis=-1, keepdims=True)
    zc = z - mu
    var = jnp.mean(zc * zc, axis=-1, keepdims=True)
    return zc * lax.rsqrt(var + NORM_EPS) * g + b


def _rms_norm(x, g):
    return x * lax.rsqrt(jnp.mean(x * x, axis=-1, keepdims=True) + NORM_EPS) * g


def _l0_proj_kernel(x_ref, wn_ref, wkat_ref, gq_ref, gkv_ref, wqupt_ref, wkn_ref, wvt_ref,
                    cq_ref, sq_ref, cqt_ref, sqt_ref,
                    qat_ref, ka_ref, vat_ref, qt_ref, kc_ref, vt_ref):
    xb = x_ref[0].astype(BF16)
    hn = jnp.dot(xb, wn_ref[...], preferred_element_type=F32)
    tm = xb.shape[0]
    ka_ref[0] = hn[:, 0:NA_WIDTH].astype(BF16)
    hat = lax.dot_general(wkat_ref[...], xb, _NT, preferred_element_type=F32)
    ones_rows = jnp.where(lax.broadcasted_iota(jnp.int32, (BF16_ROWS, tm), 0) == 0, 1.0, 0.0).astype(BF16)
    for p in range(NA_HEADS // 2):
        qat_ref[0, p] = (hat[p * LANES:(p + 1) * LANES] * (HEAD_DIM ** -0.5)).astype(BF16)
    for h in range(NA_HEADS):
        vat_ref[0, h, 0:HEAD_DIM, :] = hat[NA_WIDTH + h * HEAD_DIM:NA_WIDTH + (h + 1) * HEAD_DIM].astype(BF16)
        vat_ref[0, h, HEAD_DIM:NA_VROWS, :] = ones_rows
    o = NA_WIDTH
    qn = _rms_norm(hn[:, o:o + MLA_Q_RANK], gq_ref[...]).astype(BF16)
    o += MLA_Q_RANK
    kvn = _rms_norm(hn[:, o:o + MLA_KV_RANK], gkv_ref[...]).astype(BF16)
    o += MLA_KV_RANK
    kr = hn[:, o:o + LANES] * cq_ref[...] + hn[:, o + LANES:o + 2 * LANES] * sq_ref[...]
    hw = MLA_HEADS * LANES
    qqt = lax.dot_general(wqupt_ref[...], qn, _NT, preferred_element_type=F32)
    cqt = cqt_ref[...]
    sqt = sqt_ref[...]
    kn = jnp.dot(kvn, wkn_ref[...], preferred_element_type=F32)
    vt = lax.dot_general(wvt_ref[...], kvn, _NT, preferred_element_type=F32)
    for h in range(MLA_HEADS):
        a = qqt[h * LANES:(h + 1) * LANES]
        b = qqt[hw + h * LANES:hw + (h + 1) * LANES]
        qt_ref[0, h, 0] = (a * cqt + b * sqt).astype(BF16)
        kc_ref[0, :, h * LANES:(h + 1) * LANES] = (kn[:, h * LANES:(h + 1) * LANES] + kr).astype(BF16)
        vt_ref[0, h, 0, 0:MLA_V, :] = vt[h * MLA_V:(h + 1) * MLA_V].astype(BF16)
        vt_ref[0, h, 0, MLA_V:MLA_VROWS, :] = ones_rows


def _l0_proj(x, w, tabs):
    B, S, D = x.shape
    tm = PROJ_TM
    nt = S // tm
    hw = MLA_HEADS * LANES
    cq, sq, cqt, sqt = tabs
    row = lambda b, i: (b, i, 0)
    chunk = lambda b, i: (b, 0, i, 0, 0)
    out_shape = (
        jax.ShapeDtypeStruct((B, NA_HEADS // 2, LANES, S), BF16),
        jax.ShapeDtypeStruct((B, S, NA_WIDTH), BF16),
        jax.ShapeDtypeStruct((B, NA_HEADS, NA_VROWS, S), BF16),
        jax.ShapeDtypeStruct((B, MLA_HEADS, nt, LANES, tm), BF16),
        jax.ShapeDtypeStruct((B, S, hw), BF16),
        jax.ShapeDtypeStruct((B, MLA_HEADS, nt, MLA_VROWS, tm), BF16),
    )
    return pl.pallas_call(
        _l0_proj_kernel,
        grid=(B, nt),
        in_specs=[
            pl.BlockSpec((1, tm, D), row),
            _full(w['wn'].shape), _full(w['wkat'].shape), _full(w['gq'].shape), _full(w['gkv'].shape),
            _full(w['wqupt'].shape), _full(w['wkn'].shape), _full(w['wvt'].shape),
            pl.BlockSpec((tm, LANES), lambda b, i: (i, 0)),
            pl.BlockSpec((tm, LANES), lambda b, i: (i, 0)),
            pl.BlockSpec((LANES, tm), lambda b, i: (0, i)),
            pl.BlockSpec((LANES, tm), lambda b, i: (0, i)),
        ],
        out_specs=(
            pl.BlockSpec((1, NA_HEADS // 2, LANES, tm), lambda b, i: (b, 0, 0, i)),
            pl.BlockSpec((1, tm, NA_WIDTH), row),
            pl.BlockSpec((1, NA_HEADS, NA_VROWS, tm), lambda b, i: (b, 0, 0, i)),
            pl.BlockSpec((1, MLA_HEADS, 1, LANES, tm), chunk),
            pl.BlockSpec((1, tm, hw), row),
            pl.BlockSpec((1, MLA_HEADS, 1, MLA_VROWS, tm), chunk),
        ),
        out_shape=out_shape,
        compiler_params=_params(("parallel", "parallel")),
        name="l0_proj",
    )(x, w['wn'], w['wkat'], w['gq'], w['gkv'], w['wqupt'], w['wkn'], w['wvt'], cq, sq, cqt, sqt)


def _na_kernel(q_ref, k0_ref, k1_ref, k2_ref, k3_ref, v0_ref, v1_ref, v2_ref, v3_ref, z2_ref, o_ref,
               tab_scr, kw_scr, vw_scr):
    r = pl.program_id(2)
    nblk = pl.num_programs(2)
    case = jnp.where(r == 0, 0, jnp.where(r == nblk - 1, 2, 1))

    @pl.when((r <= 1) | (r == nblk - 1))
    def _():
        left = lax.broadcasted_iota(jnp.int32, (GRID_W, LANES), 1) < GRID_W

        def penalty(i, j):
            first_key = jnp.where(case == 0, max(i - NA_KH // 2, 0),
                                  jnp.where(case == 1, i, min(i + NA_KH // 2, NA_ROWS)))
            return jnp.where((j >= first_key) & (j < first_key + NA_KH), 0.0, NEG)

        for j in range(NA_WIN):
            for a in range(NA_ROWS // 2):
                pen = jnp.where(left, penalty(2 * a, j), penalty(2 * a + 1, j)).astype(F32)
                e = jnp.clip(j - 2 * a + NA_KH - 1 - (NA_KH // 2) * case, 0, 2 * NA_KH - 1)
                for hh in range(2):
                    tab_scr[hh, j * GRID_W:(j + 1) * GRID_W, a * LANES:(a + 1) * LANES] = z2_ref[hh, e] + pen

    tkp = k0_ref.shape[1]
    for t, (k_ref, v_ref) in enumerate(((k0_ref, v0_ref), (k1_ref, v1_ref), (k2_ref, v2_ref), (k3_ref, v3_ref))):
        kw_scr[t * tkp:(t + 1) * tkp, :] = k_ref[0]
        vw_scr[:, :, t * tkp:(t + 1) * tkp] = v_ref[0]
    qt = q_ref[0, 0]
    kw = kw_scr[...]
    upper = lax.broadcasted_iota(jnp.int32, qt.shape, 0) >= HEAD_DIM
    scores = []
    for hh in range(2):
        qh = jnp.where(upper == (hh == 1), qt, jnp.zeros_like(qt))
        scores.append(jnp.dot(kw, qh, preferred_element_type=F32) + tab_scr[hh])
    probs = [jnp.exp(st - jnp.max(st, axis=0, keepdims=True)).astype(BF16) for st in scores]
    for hh, pt in enumerate(probs):
        acc = jnp.dot(vw_scr[hh], pt, preferred_element_type=F32)
        o_ref[0, hh, 0] = (acc[0:HEAD_DIM] / acc[HEAD_DIM:HEAD_DIM + 1]).astype(BF16)


def _na_attention(qat, ka, vat, z2):
    B, S, _ = ka.shape
    rows = S // GRID_W
    nblk = rows // NA_ROWS
    tq = NA_ROWS * GRID_W
    tkp = (NA_WIN // 4) * GRID_W
    npieces = S // tkp
    hp_n = NA_HEADS // 2
    assert nblk >= 2 and rows % NA_ROWS == 0 and tq == OUT_TM

    def wstart(r):
        return jnp.clip(2 * r - 1, 0, npieces - 4)

    k_specs = [pl.BlockSpec((1, tkp, LANES), lambda hp, b, r, t=t: (b, wstart(r) + t, hp)) for t in range(4)]
    v_specs = [pl.BlockSpec((1, 2, NA_VROWS, tkp), lambda hp, b, r, t=t: (b, hp, 0, wstart(r) + t)) for t in range(4)]
    return pl.pallas_call(
        _na_kernel,
        grid=(hp_n, B, nblk),
        in_specs=[pl.BlockSpec((1, 1, LANES, tq), lambda hp, b, r: (b, hp, 0, r))] + k_specs + v_specs + [
            pl.BlockSpec((2, 2 * NA_KH, GRID_W, LANES), lambda hp, b, r: (hp, 0, 0, 0))],
        out_specs=pl.BlockSpec((1, 2, 1, HEAD_DIM, tq), lambda hp, b, r: (b, hp, r, 0, 0)),
        out_shape=jax.ShapeDtypeStruct((B, NA_HEADS, nblk, HEAD_DIM, tq), BF16),
        scratch_shapes=[pltpu.VMEM((2, 4 * tkp, tq), F32), pltpu.VMEM((4 * tkp, LANES), BF16),
                        pltpu.VMEM((2, NA_VROWS, 4 * tkp), BF16)],
        compiler_params=_params(("arbitrary", "arbitrary", "arbitrary")),
        name="na_attn",
    )(qat, ka, ka, ka, ka, vat, vat, vat, vat, z2)


def _na_bias_tiles(rpb):
    cq = np.arange(GRID_W)[None, :]
    kc = np.arange(GRID_W)[:, None]
    cs = np.clip(cq - NA_KW // 2, 0, GRID_W - NA_KW)
    col_ok = (kc >= cs) & (kc < cs + NA_KW)
    onehot = ((kc - cq + NA_KW - 1)[None] == np.arange(2 * NA_KW - 1)[:, None, None]) & col_ok[None]
    y = jnp.einsum('hab,bkq->hakq', rpb.astype(F32), jnp.asarray(onehot, F32), precision=lax.Precision.HIGHEST)
    y = jnp.where(col_ok[None, None], y, NEG)
    y = jnp.pad(y, ((0, 0), (1, 1), (0, 0), (0, 0)), constant_values=NEG)
    return jnp.concatenate([y[:, 1:], y[:, :-1]], axis=-1)


def _mla_kernel(q_ref, k_ref, v_ref, o_ref, s_scr, *, nq, nk, tk, tq, nsub, unroll, c):
    per_chunk = tk // tq

    def q_body(qi, carry):
        where = [(qi * (nsub // per_chunk) + j // per_chunk, (j % per_chunk) * tq) for j in range(nsub)]
        qts = [q_ref[0, 0, ci, :, lo:lo + tq] for ci, lo in where]

        def scores(kc, slot):
            k = k_ref[0, pl.ds(pl.multiple_of(kc * tk, tk), tk), :]
            for j, qt in enumerate(qts):
                s_scr[slot, j] = jnp.dot(k, qt, preferred_element_type=F32)

        def fold(kc, slot, mc):
            v = v_ref[0, 0, kc]
            stats = []
            for j, (m_old, _) in enumerate(mc):
                st = s_scr[slot, j]
                m_new = jnp.maximum(m_old, jnp.max(st, axis=0, keepdims=True))
                alpha = jnp.exp2((m_old - m_new) * c)
                pt = jnp.exp2((st - m_new) * c).astype(BF16)
                stats.append((m_new, alpha, pt))
            return tuple((m_new, acc * alpha + jnp.dot(v, pt, preferred_element_type=F32))
                         for (_, acc), (m_new, alpha, pt) in zip(mc, stats))

        def step(kc, slot, mc):
            k = k_ref[0, pl.ds(pl.multiple_of((kc + 1) * tk, tk), tk), :]
            v = v_ref[0, 0, kc]
            out = []
            for j, (m_old, acc) in enumerate(mc):
                st = s_scr[slot, j]
                m_new = jnp.maximum(m_old, jnp.max(st, axis=0, keepdims=True))
                alpha = jnp.exp2((m_old - m_new) * c)
                pt = jnp.exp2((st - m_new) * c).astype(BF16)
                s_scr[1 - slot, j] = jnp.dot(k, qts[j], preferred_element_type=F32)
                out.append((m_new, acc * alpha + jnp.dot(v, pt, preferred_element_type=F32)))
            return tuple(out)

        def k_body(i, mc):
            return step(2 * i + 1, 1, step(2 * i, 0, mc))

        init = tuple((jnp.full((1, tq), -jnp.inf, F32), jnp.zeros((MLA_VROWS, tq), F32)) for _ in range(nsub))
        scores(0, 0)
        mc = lax.fori_loop(0, nk // 2 - 1, k_body, init, unroll=unroll)
        res = fold(nk - 1, 1, step(nk - 2, 0, mc))
        for (_, acc), (ci, lo) in zip(res, where):
            o_ref[0, 0, ci, :, lo:lo + tq] = (acc[0:MLA_V] / acc[MLA_V:MLA_V + 1]).astype(BF16)
        return carry

    lax.fori_loop(0, nq, q_body, 0)


def _mla_attention(qt, kc, vt):
    B, H, nt, _, tm = qt.shape
    S = nt * tm
    c = float((MLA_NOPE + MLA_ROPE) ** -0.5 * np.log2(np.e))
    head = lambda b, h: (b, h, 0, 0, 0)
    nsub = MLA_NSUB
    assert nsub % (tm // MLA_TQ) == 0 and (nt * (tm // MLA_TQ)) % nsub == 0 and nt % 2 == 0
    return pl.pallas_call(
        functools.partial(_mla_kernel, nq=nt * (tm // MLA_TQ) // nsub, nk=nt, tk=tm, tq=MLA_TQ, nsub=nsub,
                          unroll=MLA_UNROLL, c=c),
        grid=(B, H),
        in_specs=[
            pl.BlockSpec((1, 1, nt, LANES, tm), head),
            pl.BlockSpec((1, S, LANES), lambda b, h: (b, 0, h)),
            pl.BlockSpec((1, 1, nt, MLA_VROWS, tm), head),
        ],
        out_specs=pl.BlockSpec((1, 1, nt, MLA_V, tm), head),
        out_shape=jax.ShapeDtypeStruct((B, H, nt, MLA_V, tm), BF16),
        scratch_shapes=[pltpu.VMEM((2, nsub, tm, MLA_TQ), F32)],
        compiler_params=_params(("parallel", "parallel")),
        name="mla_attn",
    )(qt, kc, vt)


def _out_proj_kernel(*refs, n_act):
    x_ref = refs[0]
    a_refs = refs[1:1 + n_act]
    w_refs = refs[1 + n_act:1 + 2 * n_act]
    g_ref, b_ref, o_ref = refs[1 + 2 * n_act:]
    mix = None
    for a, w in zip(a_refs, w_refs):
        at = a[0, :, 0]
        at = at.reshape(at.shape[0] * at.shape[1], at.shape[2])
        t = lax.dot_general(at, w[...], _TN, preferred_element_type=F32)
        mix = t if mix is None else mix + t
    o_ref[...] = _layer_norm(ALPHA * x_ref[...] + mix, g_ref[...], b_ref[...])


def _out_proj(x2, acts_t, ws, g, b):
    N, D = x2.shape
    tm = OUT_TM
    row = lambda i: (i, 0)
    specs = []
    for a in acts_t:
        _, H, nt, hd, tmc = a.shape
        assert tmc == tm
        specs.append(pl.BlockSpec((1, H, 1, hd, tm), lambda i, nt=nt: (i // nt, 0, i % nt, 0, 0)))
    return pl.pallas_call(
        functools.partial(_out_proj_kernel, n_act=len(acts_t)),
        grid=(N // tm,),
        in_specs=[pl.BlockSpec((tm, D), row)] + specs + [_full(w.shape) for w in ws] + [_full(g.shape), _full(b.shape)],
        out_specs=pl.BlockSpec((tm, D), row),
        out_shape=jax.ShapeDtypeStruct((N, D), F32),
        compiler_params=_params(("parallel",)),
        name="out_proj",
    )(x2, *acts_t, *ws, g, b)


def _ffn_kernel(x_ref, xp_ref, xn_ref, wup_ref, cw_ref, cb_ref, wdn_ref, g_ref, b_ref, o_ref, xb_scr, act_scr):
    i = pl.program_id(1)
    tm = x_ref.shape[1]
    hl = FFN_HALO
    ck = FFN_CK
    xp = jnp.where(i > 0, xp_ref[0], jnp.zeros_like(xp_ref[0]))
    xn = jnp.where(i < pl.num_programs(1) - 1, xn_ref[0], jnp.zeros_like(xn_ref[0]))
    xb_scr[0:hl, :] = xp.astype(BF16)
    xb_scr[hl:hl + tm, :] = x_ref[0].astype(BF16)
    xb_scr[hl + tm:hl + tm + hl, :] = xn.astype(BF16)

    def conv(c0):
        h = jnp.dot(xb_scr[...], wup_ref[:, c0:c0 + ck], preferred_element_type=F32)
        cw = cw_ref[:, c0:c0 + ck]
        return (h[hl - 1:hl - 1 + tm] * cw[0:1] + h[hl:hl + tm] * cw[1:2] + h[hl + 1:hl + 1 + tm] * cw[2:3]
                + cb_ref[:, c0:c0 + ck])

    for c in range(D_FF // ck):
        gate = conv(c * ck)
        val = conv(D_FF + c * ck)
        act = 0.5 * gate * (1.0 + lax.erf(gate * (2.0 ** -0.5))) * val
        act_scr[:, c * ck:(c + 1) * ck] = act.astype(BF16)
    y = jnp.dot(act_scr[...], wdn_ref[...], preferred_element_type=F32)
    o_ref[0] = _layer_norm(ALPHA * x_ref[0] + y, g_ref[...], b_ref[...])


def _ffn(x, w):
    B, S, D = x.shape
    tm = min(FFN_TM, S)
    hl = FFN_HALO
    nh = S // hl
    r = tm // hl
    once = dict(pipeline_mode=pl.Buffered(1))
    return pl.pallas_call(
        _ffn_kernel,
        grid=(B, S // tm),
        in_specs=[
            pl.BlockSpec((1, tm, D), lambda b, i: (b, i, 0)),
            pl.BlockSpec((1, hl, D), lambda b, i: (b, jnp.maximum(i * r - 1, 0), 0)),
            pl.BlockSpec((1, hl, D), lambda b, i: (b, jnp.minimum((i + 1) * r, nh - 1), 0)),
            pl.BlockSpec(w['wup'].shape, lambda b, i: (0, 0), **once),
            _full(w['cw'].shape), _full(w['cb'].shape),
            pl.BlockSpec(w['wdn'].shape, lambda b, i: (0, 0), **once),
            _full(w['g'].shape), _full(w['b'].shape),
        ],
        out_specs=pl.BlockSpec((1, tm, D), lambda b, i: (b, i, 0)),
        out_shape=jax.ShapeDtypeStruct((B, S, D), F32),
        scratch_shapes=[pltpu.VMEM((tm + 2 * hl, D), BF16), pltpu.VMEM((tm, D_FF), BF16)],
        compiler_params=_params(("parallel", "parallel")),
        name="ffn",
    )(x, x, x, w['wup'], w['cw'], w['cb'], w['wdn'], w['g'], w['b'])


def _l1_proj_kernel(x_ref, wn_ref, wt_ref, ck_ref, sk_ref, cqt_ref, sqt_ref, qt_ref, k_ref, vt_ref):
    xb = x_ref[0].astype(BF16)
    tm = xb.shape[0]
    hn = jnp.dot(xb, wn_ref[...], preferred_element_type=F32)
    ht = lax.dot_general(wt_ref[...], xb, _NT, preferred_element_type=F32)
    qw = GQA_Q_HEADS * HEAD_DIM
    kw2 = GQA_KV_HEADS * LANES
    ck = ck_ref[...]
    sk = sk_ref[...]
    for g in range(GQA_KV_HEADS):
        k_ref[0, :, g * LANES:(g + 1) * LANES] = (
            hn[:, g * LANES:(g + 1) * LANES] * ck + hn[:, kw2 + g * LANES:kw2 + (g + 1) * LANES] * sk).astype(BF16)
    cqt = cqt_ref[...]
    sqt = sqt_ref[...]
    for p in range(qw // LANES):
        a = ht[p * LANES:(p + 1) * LANES]
        b = ht[qw + p * LANES:qw + (p + 1) * LANES]
        qt_ref[0, p] = ((a * cqt + b * sqt) * (HEAD_DIM ** -0.5)).astype(BF16)
    ones_rows = jnp.where(lax.broadcasted_iota(jnp.int32, (BF16_ROWS, tm), 0) == 0, 1.0, 0.0).astype(BF16)
    for g in range(GQA_KV_HEADS):
        vt_ref[0, g, 0:HEAD_DIM, :] = ht[2 * qw + g * HEAD_DIM:2 * qw + (g + 1) * HEAD_DIM].astype(BF16)
        vt_ref[0, g, HEAD_DIM:WIN_VROWS, :] = ones_rows


def _l1_proj(x, w, tabs):
    B, S, D = x.shape
    tm = PROJ_TM
    ck, sk, cqt, sqt = tabs
    npair = GQA_Q_HEADS // 2
    kw2 = GQA_KV_HEADS * LANES
    row = lambda b, i: (b, i, 0)
    return pl.pallas_call(
        _l1_proj_kernel,
        grid=(B, S // tm),
        in_specs=[
            pl.BlockSpec((1, tm, D), row), _full(w['wn'].shape), _full(w['wt'].shape),
            pl.BlockSpec((tm, LANES), lambda b, i: (i, 0)),
            pl.BlockSpec((tm, LANES), lambda b, i: (i, 0)),
            pl.BlockSpec((LANES, tm), lambda b, i: (0, i)),
            pl.BlockSpec((LANES, tm), lambda b, i: (0, i)),
        ],
        out_specs=(
            pl.BlockSpec((1, npair, LANES, tm), lambda b, i: (b, 0, 0, i)),
            pl.BlockSpec((1, tm, kw2), row),
            pl.BlockSpec((1, GQA_KV_HEADS, WIN_VROWS, tm), lambda b, i: (b, 0, 0, i)),
        ),
        out_shape=(
            jax.ShapeDtypeStruct((B, npair, LANES, S), BF16),
            jax.ShapeDtypeStruct((B, S, kw2), BF16),
            jax.ShapeDtypeStruct((B, GQA_KV_HEADS, WIN_VROWS, S), BF16),
        ),
        compiler_params=_params(("parallel", "parallel")),
        name="l1_proj",
    )(x, w['wn'], w['wt'], ck, sk, cqt, sqt)


def _win_kernel(sink_ref, q_ref, kp_ref, km_ref, kn_ref, vp_ref, vm_ref, vn_ref, o_ref, kw_scr, vw_scr):
    i = pl.program_id(1)
    last = pl.num_programs(1) - 1
    tq = q_ref.shape[3]
    hb = kp_ref.shape[1]
    tk = tq + 2 * hb
    kw_scr[0:hb, :] = kp_ref[0]
    kw_scr[hb:hb + tq, :] = km_ref[0]
    kw_scr[hb + tq:tk, :] = kn_ref[0]
    vw_scr[:, :, 0:hb] = vp_ref[0]
    vw_scr[:, :, hb:hb + tq] = vm_ref[0]
    vw_scr[:, :, hb + tq:tk] = vn_ref[0]
    kr = lax.broadcasted_iota(jnp.int32, (tk, tq), 0)
    qc = lax.broadcasted_iota(jnp.int32, (tk, tq), 1)
    ok = (jnp.abs(kr - hb - qc) <= WINDOW) & ((kr >= hb) | (i > 0)) & ((kr < hb + tq) | (i < last))
    mask = jnp.where(ok, 0.0, NEG).astype(F32)
    upper = lax.broadcasted_iota(jnp.int32, (LANES, tq), 0) >= HEAD_DIM
    group = GQA_Q_HEADS // GQA_KV_HEADS
    for g in range(GQA_KV_HEADS):
        k = kw_scr[:, g * LANES:(g + 1) * LANES]
        v = vw_scr[g]
        scores = []
        for j in range(group):
            q2 = q_ref[0, (g * group + j) // 2]
            qh = jnp.where(upper == (j % 2 == 1), q2, jnp.zeros_like(q2))
            scores.append(jnp.dot(k, qh, preferred_element_type=F32) + mask)
        probs = []
        for j, st in enumerate(scores):
            sink = sink_ref[g * group + j]
            m = jnp.maximum(jnp.max(st, axis=0, keepdims=True), sink)
            probs.append((jnp.exp(st - m).astype(BF16), jnp.exp(sink - m)))
        for j, (pt, esink) in enumerate(probs):
            acc = jnp.dot(v, pt, preferred_element_type=F32)
            o = acc[0:HEAD_DIM] / (acc[HEAD_DIM:HEAD_DIM + 1] + esink)
            o_ref[0, g * group + j, 0] = o.astype(BF16)


def _win_attention(qt, k, vt, sinks):
    B, npair, _, S = qt.shape
    tq = min(WIN_TQ, S)
    hb = WINDOW
    r = tq // hb
    nhb = S // hb
    kw2 = k.shape[2]
    tm = OUT_TM
    per = tm // tq
    prev = lambda i: jnp.maximum(i * r - 1, 0)
    nxt = lambda i: jnp.minimum((i + 1) * r, nhb - 1)
    return pl.pallas_call(
        _win_kernel,
        grid_spec=pltpu.PrefetchScalarGridSpec(
            num_scalar_prefetch=1,
            grid=(B, S // tq),
            in_specs=[
                pl.BlockSpec((1, npair, LANES, tq), lambda b, i, s: (b, 0, 0, i)),
                pl.BlockSpec((1, hb, kw2), lambda b, i, s: (b, prev(i), 0)),
                pl.BlockSpec((1, tq, kw2), lambda b, i, s: (b, i, 0)),
                pl.BlockSpec((1, hb, kw2), lambda b, i, s: (b, nxt(i), 0)),
                pl.BlockSpec((1, GQA_KV_HEADS, WIN_VROWS, hb), lambda b, i, s: (b, 0, 0, prev(i))),
                pl.BlockSpec((1, GQA_KV_HEADS, WIN_VROWS, tq), lambda b, i, s: (b, 0, 0, i)),
                pl.BlockSpec((1, GQA_KV_HEADS, WIN_VROWS, hb), lambda b, i, s: (b, 0, 0, nxt(i))),
            ],
            out_specs=pl.BlockSpec((1, GQA_Q_HEADS, 1, HEAD_DIM, tq), lambda b, i, s: (b, 0, i // per, 0, i % per)),
            scratch_shapes=[pltpu.VMEM((tq + 2 * hb, kw2), BF16),
                            pltpu.VMEM((GQA_KV_HEADS, WIN_VROWS, tq + 2 * hb), BF16)],
        ),
        out_shape=jax.ShapeDtypeStruct((B, GQA_Q_HEADS, S // tm, HEAD_DIM, tm), BF16),
        compiler_params=_params(("parallel", "parallel")),
        name="win_attn",
    )(sinks, qt, k, k, k, vt, vt, vt)


def _rope_tables(S, theta, rot_dim):
    inv = 1.0 / (theta ** (jnp.arange(0, rot_dim, 2, dtype=F32) / rot_dim))
    ang = jnp.arange(S, dtype=F32)[:, None] * inv[None, :]
    cos, sin = jnp.cos(ang), jnp.sin(ang)
    return jnp.concatenate([cos, cos], axis=1), jnp.concatenate([-sin, sin], axis=1)


def _swap_halves_cols(w, rot_dim):
    half = rot_dim // 2
    return jnp.concatenate([w[:, half:rot_dim], w[:, :half]], axis=1)


def _prep_l0(w_in, g_q, w_q_up, g_kv, w_kv_up, w_out):
    d = w_in.shape[0]
    s = np.cumsum([0, NA_WIDTH, NA_WIDTH, NA_WIDTH, MLA_Q_RANK, MLA_KV_RANK, MLA_ROPE])
    w_qa, w_ka, w_va, w_ql, w_kvl, w_kr = [w_in[:, s[k]:s[k + 1]] for k in range(6)]
    H = MLA_HEADS
    qd = MLA_NOPE + MLA_ROPE

    def rope_block(w):
        return jnp.concatenate([jnp.zeros((d, MLA_NOPE), F32), w, jnp.zeros((d, LANES - qd), F32)], axis=1)

    wn = jnp.concatenate([w_ka, w_ql, w_kvl, rope_block(w_kr), rope_block(_swap_halves_cols(w_kr, MLA_ROPE))],
                         axis=1).astype(BF16)
    wq = w_q_up.reshape(MLA_Q_RANK, H, qd)
    pad = jnp.zeros((MLA_Q_RANK, H, LANES - qd), F32)
    wq_main = jnp.concatenate([wq, pad], axis=2)
    wq_sw = jnp.concatenate([jnp.zeros((MLA_Q_RANK, H, MLA_NOPE), F32),
                             wq[:, :, MLA_NOPE + MLA_ROPE // 2:], wq[:, :, MLA_NOPE:MLA_NOPE + MLA_ROPE // 2],
                             pad], axis=2)
    wqupt = jnp.concatenate([wq_main.reshape(MLA_Q_RANK, H * LANES), wq_sw.reshape(MLA_Q_RANK, H * LANES)],
                            axis=1).T.astype(BF16)
    wkv = w_kv_up.reshape(MLA_KV_RANK, H, MLA_NOPE + MLA_V)
    wkn = jnp.concatenate([wkv[:, :, :MLA_NOPE], jnp.zeros((MLA_KV_RANK, H, LANES - MLA_NOPE), F32)],
                          axis=2).reshape(MLA_KV_RANK, H * LANES).astype(BF16)
    wvt = wkv[:, :, MLA_NOPE:].reshape(MLA_KV_RANK, H * MLA_V).T.astype(BF16)
    wkat = jnp.concatenate([w_qa, w_va], axis=1).T.astype(BF16)
    return dict(wn=wn, wkat=wkat, gq=g_q.reshape(1, -1), gkv=g_kv.reshape(1, -1), wqupt=wqupt,
                wkn=wkn, wvt=wvt, woa=w_out[:NA_WIDTH].astype(BF16), wob=w_out[NA_WIDTH:].astype(BF16))


def _l0_tables(S):
    cosf, sins = _rope_tables(S, MLA_THETA, MLA_ROPE)
    ones = jnp.ones((S, MLA_NOPE), F32)
    zc = jnp.zeros((S, LANES - MLA_NOPE - MLA_ROPE), F32)
    cq = jnp.concatenate([ones, cosf, zc], axis=1)
    sq = jnp.concatenate([jnp.zeros((S, MLA_NOPE), F32), sins, zc], axis=1)
    return cq, sq, cq.T, sq.T


def _prep_l1(w_in, w_out):
    qw = GQA_Q_HEADS * HEAD_DIM
    kw = GQA_KV_HEADS * HEAD_DIM
    wq, wk, wv = w_in[:, :qw], w_in[:, qw:qw + kw], w_in[:, qw + kw:]
    d = w_in.shape[0]

    def partner(w, heads):
        wh = w.reshape(d, heads, HEAD_DIM)
        half = ROT_DIM // 2
        return jnp.concatenate([wh[:, :, half:ROT_DIM], wh[:, :, :half],
                                jnp.zeros((d, heads, HEAD_DIM - ROT_DIM), F32)], axis=2).reshape(d, heads * HEAD_DIM)

    def dup(w):
        return jnp.concatenate([w.reshape(d, GQA_KV_HEADS, HEAD_DIM)] * 2, axis=2).reshape(d, GQA_KV_HEADS * LANES)

    wn = jnp.concatenate([dup(wk), dup(partner(wk, GQA_KV_HEADS))], axis=1).astype(BF16)
    wt = jnp.concatenate([wq, partner(wq, GQA_Q_HEADS), wv], axis=1).T.astype(BF16)
    return dict(wn=wn, wt=wt, wo=w_out.astype(BF16))


def _l1_tables(S):
    cosf, sins = _rope_tables(S, ROPE_THETA, ROT_DIM)
    c64 = jnp.concatenate([cosf, jnp.ones((S, HEAD_DIM - ROT_DIM), F32)], axis=1)
    s64 = jnp.concatenate([sins, jnp.zeros((S, HEAD_DIM - ROT_DIM), F32)], axis=1)
    c128, s128 = jnp.concatenate([c64, c64], axis=1), jnp.concatenate([s64, s64], axis=1)
    return c128, s128, c128.T, s128.T


def _prep_ffn(w_up, conv_w, conv_b, w_down, g, b):
    return dict(wup=w_up.astype(BF16), cw=conv_w, cb=conv_b.reshape(1, -1),
                wdn=w_down.astype(BF16), g=g.reshape(1, -1), b=b.reshape(1, -1))


def _encoder(x, p0, na_tiles, ln0, f0, p1, sinks, ln1, f1):
    B, S, D = x.shape
    qat, ka, vat, qt, kc, vt = _l0_proj(x, p0, _l0_tables(S))
    a_out_t = _na_attention(qat, ka, vat, na_tiles)
    b_out_t = _mla_attention(qt, kc, vt)
    x = _out_proj(x.reshape(B * S, D), [a_out_t, b_out_t], [p0['woa'], p0['wob']], *ln0).reshape(B, S, D)
    x = _ffn(x, f0)
    qt1, k1, vt1 = _l1_proj(x, p1, _l1_tables(S))
    attn_t = _win_attention(qt1, k1, vt1, sinks)
    x = _out_proj(x.reshape(B * S, D), [attn_t], [p1['wo']], *ln1).reshape(B, S, D)
    return _ffn(x, f1)


def kernel(x_prompt, x_sample, l0_w_in, l0_rpb, l0_g_q_norm, l0_w_q_up, l0_g_kv_norm, l0_w_kv_up, l0_w_out, l0_ln1_g, l0_ln1_b, l0_ffn_w_up, l0_ffn_conv_w, l0_ffn_conv_b, l0_ffn_w_down, l0_ln2_g, l0_ln2_b, l1_w_in, l1_sinks, l1_w_out, l1_ln1_g, l1_ln1_b, l1_ffn_w_up, l1_ffn_conv_w, l1_ffn_conv_b, l1_ffn_w_down, l1_ln2_g, l1_ln2_b):
    p0 = _prep_l0(l0_w_in, l0_g_q_norm, l0_w_q_up, l0_g_kv_norm, l0_w_kv_up, l0_w_out)
    na_tiles = _na_bias_tiles(l0_rpb)
    ln0 = (l0_ln1_g.reshape(1, -1), l0_ln1_b.reshape(1, -1))
    f0 = _prep_ffn(l0_ffn_w_up, l0_ffn_conv_w, l0_ffn_conv_b, l0_ffn_w_down, l0_ln2_g, l0_ln2_b)
    p1 = _prep_l1(l1_w_in, l1_w_out)
    ln1 = (l1_ln1_g.reshape(1, -1), l1_ln1_b.reshape(1, -1))
    f1 = _prep_ffn(l1_ffn_w_up, l1_ffn_conv_w, l1_ffn_conv_b, l1_ffn_w_down, l1_ln2_g, l1_ln2_b)
    args = (p0, na_tiles, ln0, f0, p1, l1_sinks.astype(F32), ln1, f1)
    return (_encoder(x_prompt, *args), _encoder(x_sample, *args))
```

```python
import functools

import numpy as np
import jax
import jax.numpy as jnp
from jax import lax
from jax.experimental import pallas as pl
from jax.experimental.pallas import tpu as pltpu

F32 = jnp.float32
BF16 = jnp.bfloat16

D_MODEL = 1024
DEPTH = 2
GRID_W = 64
HEAD_DIM = 64
NA_HEADS = 8
NA_KH = 8
NA_KW = 16
MLA_HEADS = 8
MLA_Q_RANK = 384
MLA_KV_RANK = 256
MLA_NOPE = 64
MLA_ROPE = 32
MLA_V = 64
MLA_THETA = 10000.0
GQA_Q_HEADS = 16
GQA_KV_HEADS = 4
WINDOW = 128
ROPE_THETA = 500000.0
ROT_DIM = HEAD_DIM // 4
D_FF = 2816
ALPHA = (2 * DEPTH) ** 0.25
NORM_EPS = 1e-5
NA_WIDTH = NA_HEADS * HEAD_DIM

LANES = 128
BF16_ROWS = 16
NEG = -1e30
VMEM_LIMIT = 56 * 1024 * 1024

PROJ_TM = 512
OUT_TM = PROJ_TM
FFN_TM = 1024
FFN_CK = 256
FFN_HALO = BF16_ROWS
MLA_TQ = 256
MLA_NSUB = 4
MLA_UNROLL = 2
MLA_VROWS = MLA_V + BF16_ROWS
NA_ROWS = 8
NA_WIN = 16
NA_VROWS = HEAD_DIM + BF16_ROWS
WIN_TQ = 256
WIN_AHEAD = 4
WIN_VROWS = HEAD_DIM + BF16_ROWS

_NT = (((1,), (1,)), ((), ()))
_TN = (((0,), (0,)), ((), ()))


def _params(sem):
    return pltpu.CompilerParams(dimension_semantics=sem, vmem_limit_bytes=VMEM_LIMIT)


def _full(shape):
    n = len(shape)
    return pl.BlockSpec(shape, lambda *_: (0,) * n)


def _layer_norm(z, g, b):
    mu = jnp.mean(z, axis=-1, keepdims=True)
    zc = z - mu
    var = jnp.mean(zc * zc, axis=-1, keepdims=True)
    return zc * lax.rsqrt(var + NORM_EPS) * g + b


def _rms_norm(x, g):
    return x * lax.rsqrt(jnp.mean(x * x, axis=-1, keepdims=True) + NORM_EPS) * g


def _l0_proj_kernel(x_ref, wn_ref, wkat_ref, gq_ref, gkv_ref, wqupt_ref, wkn_ref, wvt_ref,
                    cq_ref, sq_ref, cqt_ref, sqt_ref,
                    qat_ref, ka_ref, vat_ref, qt_ref, kc_ref, vt_ref):
    xb = x_ref[0].astype(BF16)
    hn = jnp.dot(xb, wn_ref[...], preferred_element_type=F32)
    tm = xb.shape[0]
    ka_ref[0] = hn[:, 0:NA_WIDTH].astype(BF16)
    hat = lax.dot_general(wkat_ref[...], xb, _NT, preferred_element_type=F32)
    ones_rows = jnp.where(lax.broadcasted_iota(jnp.int32, (BF16_ROWS, tm), 0) == 0, 1.0, 0.0).astype(BF16)
    for p in range(NA_HEADS // 2):
        qat_ref[0, p] = (hat[p * LANES:(p + 1) * LANES] * (HEAD_DIM ** -0.5)).astype(BF16)
    for h in range(NA_HEADS):
        vat_ref[0, h, 0:HEAD_DIM, :] = hat[NA_WIDTH + h * HEAD_DIM:NA_WIDTH + (h + 1) * HEAD_DIM].astype(BF16)
        vat_ref[0, h, HEAD_DIM:NA_VROWS, :] = ones_rows
    o = NA_WIDTH
    qn = _rms_norm(hn[:, o:o + MLA_Q_RANK], gq_ref[...]).astype(BF16)
    o += MLA_Q_RANK
    kvn = _rms_norm(hn[:, o:o + MLA_KV_RANK], gkv_ref[...]).astype(BF16)
    o += MLA_KV_RANK
    kr = hn[:, o:o + LANES] * cq_ref[...] + hn[:, o + LANES:o + 2 * LANES] * sq_ref[...]
    hw = MLA_HEADS * LANES
    qqt = lax.dot_general(wqupt_ref[...], qn, _NT, preferred_element_type=F32)
    cqt = cqt_ref[...]
    sqt = sqt_ref[...]
    kn = jnp.dot(kvn, wkn_ref[...], preferred_element_type=F32)
    vt = lax.dot_general(wvt_ref[...], kvn, _NT, preferred_element_type=F32)
    for h in range(MLA_HEADS):
        a = qqt[h * LANES:(h + 1) * LANES]
        b = qqt[hw + h * LANES:hw + (h + 1) * LANES]
        qt_ref[0, h, 0] = (a * cqt + b * sqt).astype(BF16)
        kc_ref[0, :, h * LANES:(h + 1) * LANES] = (kn[:, h * LANES:(h + 1) * LANES] + kr).astype(BF16)
        vt_ref[0, h, 0, 0:MLA_V, :] = vt[h * MLA_V:(h + 1) * MLA_V].astype(BF16)
        vt_ref[0, h, 0, MLA_V:MLA_VROWS, :] = ones_rows


def _l0_proj(x, w, tabs):
    B, S, D = x.shape
    tm = PROJ_TM
    nt = S // tm
    hw = MLA_HEADS * LANES
    cq, sq, cqt, sqt = tabs
    row = lambda b, i: (b, i, 0)
    chunk = lambda b, i: (b, 0, i, 0, 0)
    out_shape = (
        jax.ShapeDtypeStruct((B, NA_HEADS // 2, LANES, S), BF16),
        jax.ShapeDtypeStruct((B, S, NA_WIDTH), BF16),
        jax.ShapeDtypeStruct((B, NA_HEADS, NA_VROWS, S), BF16),
        jax.ShapeDtypeStruct((B, MLA_HEADS, nt, LANES, tm), BF16),
        jax.ShapeDtypeStruct((B, S, hw), BF16),
        jax.ShapeDtypeStruct((B, MLA_HEADS, nt, MLA_VROWS, tm), BF16),
    )
    return pl.pallas_call(
        _l0_proj_kernel,
        grid=(B, nt),
        in_specs=[
            pl.BlockSpec((1, tm, D), row),
            _full(w['wn'].shape), _full(w['wkat'].shape), _full(w['gq'].shape), _full(w['gkv'].shape),
            _full(w['wqupt'].shape), _full(w['wkn'].shape), _full(w['wvt'].shape),
            pl.BlockSpec((tm, LANES), lambda b, i: (i, 0)),
            pl.BlockSpec((tm, LANES), lambda b, i: (i, 0)),
            pl.BlockSpec((LANES, tm), lambda b, i: (0, i)),
            pl.BlockSpec((LANES, tm), lambda b, i: (0, i)),
        ],
        out_specs=(
            pl.BlockSpec((1, NA_HEADS // 2, LANES, tm), lambda b, i: (b, 0, 0, i)),
            pl.BlockSpec((1, tm, NA_WIDTH), row),
            pl.BlockSpec((1, NA_HEADS, NA_VROWS, tm), lambda b, i: (b, 0, 0, i)),
            pl.BlockSpec((1, MLA_HEADS, 1, LANES, tm), chunk),
            pl.BlockSpec((1, tm, hw), row),
            pl.BlockSpec((1, MLA_HEADS, 1, MLA_VROWS, tm), chunk),
        ),
        out_shape=out_shape,
        compiler_params=_params(("parallel", "parallel")),
        name="l0_proj",
    )(x, w['wn'], w['wkat'], w['gq'], w['gkv'], w['wqupt'], w['wkn'], w['wvt'], cq, sq, cqt, sqt)


def _na_kernel(q_ref, k0_ref, k1_ref, k2_ref, k3_ref, v0_ref, v1_ref, v2_ref, v3_ref, z2_ref, o_ref,
               tab_scr, kw_scr, vw_scr):
    r = pl.program_id(2)
    nblk = pl.num_programs(2)
    case = jnp.where(r == 0, 0, jnp.where(r == nblk - 1, 2, 1))

    @pl.when((r <= 1) | (r == nblk - 1))
    def _():
        left = lax.broadcasted_iota(jnp.int32, (GRID_W, LANES), 1) < GRID_W

        def penalty(i, j):
            first_key = jnp.where(case == 0, max(i - NA_KH // 2, 0),
                                  jnp.where(case == 1, i, min(i + NA_KH // 2, NA_ROWS)))
            return jnp.where((j >= first_key) & (j < first_key + NA_KH), 0.0, NEG)

        for j in range(NA_WIN):
            for a in range(NA_ROWS // 2):
                pen = jnp.where(left, penalty(2 * a, j), penalty(2 * a + 1, j)).astype(F32)
                e = jnp.clip(j - 2 * a + NA_KH - 1 - (NA_KH // 2) * case, 0, 2 * NA_KH - 1)
                for hh in range(2):
                    tab_scr[hh, j * GRID_W:(j + 1) * GRID_W, a * LANES:(a + 1) * LANES] = z2_ref[hh, e] + pen

    tkp = k0_ref.shape[1]
    for t, (k_ref, v_ref) in enumerate(((k0_ref, v0_ref), (k1_ref, v1_ref), (k2_ref, v2_ref), (k3_ref, v3_ref))):
        kw_scr[t * tkp:(t + 1) * tkp, :] = k_ref[0]
        vw_scr[:, :, t * tkp:(t + 1) * tkp] = v_ref[0]
    qt = q_ref[0, 0]
    kw = kw_scr[...]
    upper = lax.broadcasted_iota(jnp.int32, qt.shape, 0) >= HEAD_DIM
    scores = []
    for hh in range(2):
        qh = jnp.where(upper == (hh == 1), qt, jnp.zeros_like(qt))
        scores.append(jnp.dot(kw, qh, preferred_element_type=F32) + tab_scr[hh])
    probs = [jnp.exp(st - jnp.max(st, axis=0, keepdims=True)).astype(BF16) for st in scores]
    for hh, pt in enumerate(probs):
        acc = jnp.dot(vw_scr[hh], pt, preferred_element_type=F32)
        o_ref[0, hh, 0] = (acc[0:HEAD_DIM] / acc[HEAD_DIM:HEAD_DIM + 1]).astype(BF16)


def _na_attention(qat, ka, vat, z2):
    B, S, _ = ka.shape
    rows = S // GRID_W
    nblk = rows // NA_ROWS
    tq = NA_ROWS * GRID_W
    tkp = (NA_WIN // 4) * GRID_W
    npieces = S // tkp
    hp_n = NA_HEADS // 2
    assert nblk >= 2 and rows % NA_ROWS == 0 and tq == OUT_TM

    def wstart(r):
        return jnp.clip(2 * r - 1, 0, npieces - 4)

    k_specs = [pl.BlockSpec((1, tkp, LANES), lambda hp, b, r, t=t: (b, wstart(r) + t, hp)) for t in range(4)]
    v_specs = [pl.BlockSpec((1, 2, NA_VROWS, tkp), lambda hp, b, r, t=t: (b, hp, 0, wstart(r) + t)) for t in range(4)]
    return pl.pallas_call(
        _na_kernel,
        grid=(hp_n, B, nblk),
        in_specs=[pl.BlockSpec((1, 1, LANES, tq), lambda hp, b, r: (b, hp, 0, r))] + k_specs + v_specs + [
            pl.BlockSpec((2, 2 * NA_KH, GRID_W, LANES), lambda hp, b, r: (hp, 0, 0, 0))],
        out_specs=pl.BlockSpec((1, 2, 1, HEAD_DIM, tq), lambda hp, b, r: (b, hp, r, 0, 0)),
        out_shape=jax.ShapeDtypeStruct((B, NA_HEADS, nblk, HEAD_DIM, tq), BF16),
        scratch_shapes=[pltpu.VMEM((2, 4 * tkp, tq), F32), pltpu.VMEM((4 * tkp, LANES), BF16),
                        pltpu.VMEM((2, NA_VROWS, 4 * tkp), BF16)],
        compiler_params=_params(("arbitrary", "arbitrary", "arbitrary")),
        name="na_attn",
    )(qat, ka, ka, ka, ka, vat, vat, vat, vat, z2)


def _na_bias_tiles(rpb):
    cq = np.arange(GRID_W)[None, :]
    kc = np.arange(GRID_W)[:, None]
    cs = np.clip(cq - NA_KW // 2, 0, GRID_W - NA_KW)
    col_ok = (kc >= cs) & (kc < cs + NA_KW)
    onehot = ((kc - cq + NA_KW - 1)[None] == np.arange(2 * NA_KW - 1)[:, None, None]) & col_ok[None]
    y = jnp.einsum('hab,bkq->hakq', rpb.astype(F32), jnp.asarray(onehot, F32), precision=lax.Precision.HIGHEST)
    y = jnp.where(col_ok[None, None], y, NEG)
    y = jnp.pad(y, ((0, 0), (1, 1), (0, 0), (0, 0)), constant_values=NEG)
    return jnp.concatenate([y[:, 1:], y[:, :-1]], axis=-1)


def _mla_kernel(q_ref, k_ref, v_ref, o_ref, s_scr, *, nq, nk, tk, tq, nsub, unroll, c):
    per_chunk = tk // tq

    def q_body(qi, carry):
        where = [(qi * (nsub // per_chunk) + j // per_chunk, (j % per_chunk) * tq) for j in range(nsub)]
        qts = [q_ref[0, 0, ci, :, lo:lo + tq] for ci, lo in where]

        def scores(kc, slot):
            k = k_ref[0, pl.ds(pl.multiple_of(kc * tk, tk), tk), :]
            for j, qt in enumerate(qts):
                s_scr[slot, j] = jnp.dot(k, qt, preferred_element_type=F32)

        def fold(kc, slot, mc):
            v = v_ref[0, 0, kc]
            stats = []
            for j, (m_old, _) in enumerate(mc):
                st = s_scr[slot, j]
                m_new = jnp.maximum(m_old, jnp.max(st, axis=0, keepdims=True))
                alpha = jnp.exp2((m_old - m_new) * c)
                pt = jnp.exp2((st - m_new) * c).astype(BF16)
                stats.append((m_new, alpha, pt))
            return tuple((m_new, acc * alpha + jnp.dot(v, pt, preferred_element_type=F32))
                         for (_, acc), (m_new, alpha, pt) in zip(mc, stats))

        def step(kc, slot, mc):
            k = k_ref[0, pl.ds(pl.multiple_of((kc + 1) * tk, tk), tk), :]
            v = v_ref[0, 0, kc]
            out = []
            for j, (m_old, acc) in enumerate(mc):
                st = s_scr[slot, j]
                m_new = jnp.maximum(m_old, jnp.max(st, axis=0, keepdims=True))
                alpha = jnp.exp2((m_old - m_new) * c)
                pt = jnp.exp2((st - m_new) * c).astype(BF16)
                s_scr[1 - slot, j] = jnp.dot(k, qts[j], preferred_element_type=F32)
                out.append((m_new, acc * alpha + jnp.dot(v, pt, preferred_element_type=F32)))
            return tuple(out)

        def k_body(i, mc):
            return step(2 * i + 1, 1, step(2 * i, 0, mc))

        init = tuple((jnp.full((1, tq), -jnp.inf, F32), jnp.zeros((MLA_VROWS, tq), F32)) for _ in range(nsub))
        scores(0, 0)
        mc = lax.fori_loop(0, nk // 2 - 1, k_body, init, unroll=unroll)
        res = fold(nk - 1, 1, step(nk - 2, 0, mc))
        for (_, acc), (ci, lo) in zip(res, where):
            o_ref[0, 0, ci, :, lo:lo + tq] = (acc[0:MLA_V] / acc[MLA_V:MLA_V + 1]).astype(BF16)
        return carry

    lax.fori_loop(0, nq, q_body, 0)


def _mla_attention(qt, kc, vt):
    B, H, nt, _, tm = qt.shape
    S = nt * tm
    c = float((MLA_NOPE + MLA_ROPE) ** -0.5 * np.log2(np.e))
    head = lambda b, h: (b, h, 0, 0, 0)
    nsub = MLA_NSUB
    assert nsub % (tm // MLA_TQ) == 0 and (nt * (tm // MLA_TQ)) % nsub == 0 and nt % 2 == 0
    return pl.pallas_call(
        functools.partial(_mla_kernel, nq=nt * (tm // MLA_TQ) // nsub, nk=nt, tk=tm, tq=MLA_TQ, nsub=nsub,
                          unroll=MLA_UNROLL * 2 if nt >= 32 else MLA_UNROLL, c=c),
        grid=(B, H),
        in_specs=[
            pl.BlockSpec((1, 1, nt, LANES, tm), head),
            pl.BlockSpec((1, S, LANES), lambda b, h: (b, 0, h)),
            pl.BlockSpec((1, 1, nt, MLA_VROWS, tm), head),
        ],
        out_specs=pl.BlockSpec((1, 1, nt, MLA_V, tm), head),
        out_shape=jax.ShapeDtypeStruct((B, H, nt, MLA_V, tm), BF16),
        scratch_shapes=[pltpu.VMEM((2, nsub, tm, MLA_TQ), F32)],
        compiler_params=_params(("parallel", "parallel")),
        name="mla_attn",
    )(qt, kc, vt)


def _out_proj_kernel(*refs, n_act):
    x_ref = refs[0]
    a_refs = refs[1:1 + n_act]
    w_refs = refs[1 + n_act:1 + 2 * n_act]
    g_ref, b_ref, o_ref = refs[1 + 2 * n_act:]
    mix = None
    for a, w in zip(a_refs, w_refs):
        at = a[0, :, 0]
        at = at.reshape(at.shape[0] * at.shape[1], at.shape[2])
        t = lax.dot_general(at, w[...], _TN, preferred_element_type=F32)
        mix = t if mix is None else mix + t
    o_ref[...] = _layer_norm(ALPHA * x_ref[...] + mix, g_ref[...], b_ref[...])


def _out_proj(x2, acts_t, ws, g, b):
    N, D = x2.shape
    tm = OUT_TM
    row = lambda i: (i, 0)
    specs = []
    for a in acts_t:
        _, H, nt, hd, tmc = a.shape
        assert tmc == tm
        specs.append(pl.BlockSpec((1, H, 1, hd, tm), lambda i, nt=nt: (i // nt, 0, i % nt, 0, 0)))
    return pl.pallas_call(
        functools.partial(_out_proj_kernel, n_act=len(acts_t)),
        grid=(N // tm,),
        in_specs=[pl.BlockSpec((tm, D), row)] + specs + [_full(w.shape) for w in ws] + [_full(g.shape), _full(b.shape)],
        out_specs=pl.BlockSpec((tm, D), row),
        out_shape=jax.ShapeDtypeStruct((N, D), F32),
        compiler_params=_params(("parallel",)),
        name="out_proj",
    )(x2, *acts_t, *ws, g, b)


def _ffn_kernel(x_ref, xp_ref, xn_ref, wup_ref, cw_ref, cb_ref, wdn_ref, g_ref, b_ref, o_ref, xb_scr, act_scr):
    i = pl.program_id(1)
    tm = x_ref.shape[1]
    hl = FFN_HALO
    ck = FFN_CK
    xp = jnp.where(i > 0, xp_ref[0], jnp.zeros_like(xp_ref[0]))
    xn = jnp.where(i < pl.num_programs(1) - 1, xn_ref[0], jnp.zeros_like(xn_ref[0]))
    xb_scr[0:hl, :] = xp.astype(BF16)
    xb_scr[hl:hl + tm, :] = x_ref[0].astype(BF16)
    xb_scr[hl + tm:hl + tm + hl, :] = xn.astype(BF16)

    def conv(c0):
        h = jnp.dot(xb_scr[...], wup_ref[:, c0:c0 + ck], preferred_element_type=F32)
        cw = cw_ref[:, c0:c0 + ck]
        return (h[hl - 1:hl - 1 + tm] * cw[0:1] + h[hl:hl + tm] * cw[1:2] + h[hl + 1:hl + 1 + tm] * cw[2:3]
                + cb_ref[:, c0:c0 + ck])

    for c in range(D_FF // ck):
        gate = conv(c * ck)
        val = conv(D_FF + c * ck)
        act = 0.5 * gate * (1.0 + lax.erf(gate * (2.0 ** -0.5))) * val
        act_scr[:, c * ck:(c + 1) * ck] = act.astype(BF16)
    y = jnp.dot(act_scr[...], wdn_ref[...], preferred_element_type=F32)
    o_ref[0] = _layer_norm(ALPHA * x_ref[0] + y, g_ref[...], b_ref[...])


def _ffn(x, w):
    B, S, D = x.shape
    tm = min(FFN_TM, S)
    hl = FFN_HALO
    nh = S // hl
    r = tm // hl
    once = dict(pipeline_mode=pl.Buffered(1))
    return pl.pallas_call(
        _ffn_kernel,
        grid=(B, S // tm),
        in_specs=[
            pl.BlockSpec((1, tm, D), lambda b, i: (b, i, 0)),
            pl.BlockSpec((1, hl, D), lambda b, i: (b, jnp.maximum(i * r - 1, 0), 0)),
            pl.BlockSpec((1, hl, D), lambda b, i: (b, jnp.minimum((i + 1) * r, nh - 1), 0)),
            pl.BlockSpec(w['wup'].shape, lambda b, i: (0, 0), **once),
            _full(w['cw'].shape), _full(w['cb'].shape),
            pl.BlockSpec(w['wdn'].shape, lambda b, i: (0, 0), **once),
            _full(w['g'].shape), _full(w['b'].shape),
        ],
        out_specs=pl.BlockSpec((1, tm, D), lambda b, i: (b, i, 0)),
        out_shape=jax.ShapeDtypeStruct((B, S, D), F32),
        scratch_shapes=[pltpu.VMEM((tm + 2 * hl, D), BF16), pltpu.VMEM((tm, D_FF), BF16)],
        compiler_params=_params(("parallel", "parallel")),
        name="ffn",
    )(x, x, x, w['wup'], w['cw'], w['cb'], w['wdn'], w['g'], w['b'])


def _l1_proj_kernel(x_ref, wn_ref, wt_ref, ck_ref, sk_ref, cqt_ref, sqt_ref, qt_ref, k_ref, vt_ref):
    xb = x_ref[0].astype(BF16)
    tm = xb.shape[0]
    hn = jnp.dot(xb, wn_ref[...], preferred_element_type=F32)
    ht = lax.dot_general(wt_ref[...], xb, _NT, preferred_element_type=F32)
    qw = GQA_Q_HEADS * HEAD_DIM
    kw2 = GQA_KV_HEADS * LANES
    ck = ck_ref[...]
    sk = sk_ref[...]
    for g in range(GQA_KV_HEADS):
        k_ref[0, :, g * LANES:(g + 1) * LANES] = (
            hn[:, g * LANES:(g + 1) * LANES] * ck + hn[:, kw2 + g * LANES:kw2 + (g + 1) * LANES] * sk).astype(BF16)
    cqt = cqt_ref[...]
    sqt = sqt_ref[...]
    for p in range(qw // LANES):
        a = ht[p * LANES:(p + 1) * LANES]
        b = ht[qw + p * LANES:qw + (p + 1) * LANES]
        qt_ref[0, p] = ((a * cqt + b * sqt) * (HEAD_DIM ** -0.5)).astype(BF16)
    ones_rows = jnp.where(lax.broadcasted_iota(jnp.int32, (BF16_ROWS, tm), 0) == 0, 1.0, 0.0).astype(BF16)
    for g in range(GQA_KV_HEADS):
        vt_ref[0, g, 0:HEAD_DIM, :] = ht[2 * qw + g * HEAD_DIM:2 * qw + (g + 1) * HEAD_DIM].astype(BF16)
        vt_ref[0, g, HEAD_DIM:WIN_VROWS, :] = ones_rows


def _l1_proj(x, w, tabs):
    B, S, D = x.shape
    tm = PROJ_TM
    ck, sk, cqt, sqt = tabs
    npair = GQA_Q_HEADS // 2
    kw2 = GQA_KV_HEADS * LANES
    row = lambda b, i: (b, i, 0)
    return pl.pallas_call(
        _l1_proj_kernel,
        grid=(B, S // tm),
        in_specs=[
            pl.BlockSpec((1, tm, D), row), _full(w['wn'].shape), _full(w['wt'].shape),
            pl.BlockSpec((tm, LANES), lambda b, i: (i, 0)),
            pl.BlockSpec((tm, LANES), lambda b, i: (i, 0)),
            pl.BlockSpec((LANES, tm), lambda b, i: (0, i)),
            pl.BlockSpec((LANES, tm), lambda b, i: (0, i)),
        ],
        out_specs=(
            pl.BlockSpec((1, npair, LANES, tm), lambda b, i: (b, 0, 0, i)),
            pl.BlockSpec((1, tm, kw2), row),
            pl.BlockSpec((1, GQA_KV_HEADS, WIN_VROWS, tm), lambda b, i: (b, 0, 0, i)),
        ),
        out_shape=(
            jax.ShapeDtypeStruct((B, npair, LANES, S), BF16),
            jax.ShapeDtypeStruct((B, S, kw2), BF16),
            jax.ShapeDtypeStruct((B, GQA_KV_HEADS, WIN_VROWS, S), BF16),
        ),
        compiler_params=_params(("parallel", "parallel")),
        name="l1_proj",
    )(x, w['wn'], w['wt'], ck, sk, cqt, sqt)


def _win_kernel(sink_ref, q_ref, kp_ref, km_ref, kn_ref, vp_ref, vm_ref, vn_ref, o_ref, kw_scr, vw_scr):
    i = pl.program_id(1)
    last = pl.num_programs(1) - 1
    tq = q_ref.shape[3]
    hb = kp_ref.shape[1]
    tk = tq + 2 * hb
    kw_scr[0:hb, :] = kp_ref[0]
    kw_scr[hb:hb + tq, :] = km_ref[0]
    kw_scr[hb + tq:tk, :] = kn_ref[0]
    vw_scr[:, :, 0:hb] = vp_ref[0]
    vw_scr[:, :, hb:hb + tq] = vm_ref[0]
    vw_scr[:, :, hb + tq:tk] = vn_ref[0]
    kr = lax.broadcasted_iota(jnp.int32, (tk, tq), 0)
    qc = lax.broadcasted_iota(jnp.int32, (tk, tq), 1)
    ok = (jnp.abs(kr - hb - qc) <= WINDOW) & ((kr >= hb) | (i > 0)) & ((kr < hb + tq) | (i < last))
    mask = jnp.where(ok, 0.0, NEG).astype(F32)
    upper = lax.broadcasted_iota(jnp.int32, (LANES, tq), 0) >= HEAD_DIM
    group = GQA_Q_HEADS // GQA_KV_HEADS

    def scores(h):
        g = h // group
        q2 = q_ref[0, h // 2]
        qh = jnp.where(upper == (h % 2 == 1), q2, jnp.zeros_like(q2))
        return jnp.dot(kw_scr[:, g * LANES:(g + 1) * LANES], qh, preferred_element_type=F32) + mask

    pending = [scores(h) for h in range(WIN_AHEAD)]
    for h in range(GQA_Q_HEADS):
        st = pending.pop(0)
        sink = sink_ref[h]
        m = jnp.maximum(jnp.max(st, axis=0, keepdims=True), sink)
        pt = jnp.exp(st - m).astype(BF16)
        esink = jnp.exp(sink - m)
        if h + WIN_AHEAD < GQA_Q_HEADS:
            pending.append(scores(h + WIN_AHEAD))
        acc = jnp.dot(vw_scr[h // group], pt, preferred_element_type=F32)
        o_ref[0, h, 0] = (acc[0:HEAD_DIM] / (acc[HEAD_DIM:HEAD_DIM + 1] + esink)).astype(BF16)


def _win_attention(qt, k, vt, sinks):
    B, npair, _, S = qt.shape
    tq = min(WIN_TQ, S)
    hb = WINDOW
    r = tq // hb
    nhb = S // hb
    kw2 = k.shape[2]
    tm = OUT_TM
    per = tm // tq
    prev = lambda i: jnp.maximum(i * r - 1, 0)
    nxt = lambda i: jnp.minimum((i + 1) * r, nhb - 1)
    return pl.pallas_call(
        _win_kernel,
        grid_spec=pltpu.PrefetchScalarGridSpec(
            num_scalar_prefetch=1,
            grid=(B, S // tq),
            in_specs=[
                pl.BlockSpec((1, npair, LANES, tq), lambda b, i, s: (b, 0, 0, i)),
                pl.BlockSpec((1, hb, kw2), lambda b, i, s: (b, prev(i), 0)),
                pl.BlockSpec((1, tq, kw2), lambda b, i, s: (b, i, 0)),
                pl.BlockSpec((1, hb, kw2), lambda b, i, s: (b, nxt(i), 0)),
                pl.BlockSpec((1, GQA_KV_HEADS, WIN_VROWS, hb), lambda b, i, s: (b, 0, 0, prev(i))),
                pl.BlockSpec((1, GQA_KV_HEADS, WIN_VROWS, tq), lambda b, i, s: (b, 0, 0, i)),
                pl.BlockSpec((1, GQA_KV_HEADS, WIN_VROWS, hb), lambda b, i, s: (b, 0, 0, nxt(i))),
            ],
            out_specs=pl.BlockSpec((1, GQA_Q_HEADS, 1, HEAD_DIM, tq), lambda b, i, s: (b, 0, i // per, 0, i % per)),
            scratch_shapes=[pltpu.VMEM((tq + 2 * hb, kw2), BF16),
                            pltpu.VMEM((GQA_KV_HEADS, WIN_VROWS, tq + 2 * hb), BF16)],
        ),
        out_shape=jax.ShapeDtypeStruct((B, GQA_Q_HEADS, S // tm, HEAD_DIM, tm), BF16),
        compiler_params=_params(("parallel", "parallel")),
        name="win_attn",
    )(sinks, qt, k, k, k, vt, vt, vt)


def _rope_tables(S, theta, rot_dim):
    inv = 1.0 / (theta ** (jnp.arange(0, rot_dim, 2, dtype=F32) / rot_dim))
    ang = jnp.arange(S, dtype=F32)[:, None] * inv[None, :]
    cos, sin = jnp.cos(ang), jnp.sin(ang)
    return jnp.concatenate([cos, cos], axis=1), jnp.concatenate([-sin, sin], axis=1)


def _swap_halves_cols(w, rot_dim):
    half = rot_dim // 2
    return jnp.concatenate([w[:, half:rot_dim], w[:, :half]], axis=1)


def _prep_l0(w_in, g_q, w_q_up, g_kv, w_kv_up, w_out):
    d = w_in.shape[0]
    s = np.cumsum([0, NA_WIDTH, NA_WIDTH, NA_WIDTH, MLA_Q_RANK, MLA_KV_RANK, MLA_ROPE])
    w_qa, w_ka, w_va, w_ql, w_kvl, w_kr = [w_in[:, s[k]:s[k + 1]] for k in range(6)]
    H = MLA_HEADS
    qd = MLA_NOPE + MLA_ROPE

    def rope_block(w):
        return jnp.concatenate([jnp.zeros((d, MLA_NOPE), F32), w, jnp.zeros((d, LANES - qd), F32)], axis=1)

    wn = jnp.concatenate([w_ka, w_ql, w_kvl, rope_block(w_kr), rope_block(_swap_halves_cols(w_kr, MLA_ROPE))],
                         axis=1).astype(BF16)
    wq = w_q_up.reshape(MLA_Q_RANK, H, qd)
    pad = jnp.zeros((MLA_Q_RANK, H, LANES - qd), F32)
    wq_main = jnp.concatenate([wq, pad], axis=2)
    wq_sw = jnp.concatenate([jnp.zeros((MLA_Q_RANK, H, MLA_NOPE), F32),
                             wq[:, :, MLA_NOPE + MLA_ROPE // 2:], wq[:, :, MLA_NOPE:MLA_NOPE + MLA_ROPE // 2],
                             pad], axis=2)
    wqupt = jnp.concatenate([wq_main.reshape(MLA_Q_RANK, H * LANES), wq_sw.reshape(MLA_Q_RANK, H * LANES)],
                            axis=1).T.astype(BF16)
    wkv = w_kv_up.reshape(MLA_KV_RANK, H, MLA_NOPE + MLA_V)
    wkn = jnp.concatenate([wkv[:, :, :MLA_NOPE], jnp.zeros((MLA_KV_RANK, H, LANES - MLA_NOPE), F32)],
                          axis=2).reshape(MLA_KV_RANK, H * LANES).astype(BF16)
    wvt = wkv[:, :, MLA_NOPE:].reshape(MLA_KV_RANK, H * MLA_V).T.astype(BF16)
    wkat = jnp.concatenate([w_qa, w_va], axis=1).T.astype(BF16)
    return dict(wn=wn, wkat=wkat, gq=g_q.reshape(1, -1), gkv=g_kv.reshape(1, -1), wqupt=wqupt,
                wkn=wkn, wvt=wvt, woa=w_out[:NA_WIDTH].astype(BF16), wob=w_out[NA_WIDTH:].astype(BF16))


def _l0_tables(S):
    cosf, sins = _rope_tables(S, MLA_THETA, MLA_ROPE)
    ones = jnp.ones((S, MLA_NOPE), F32)
    zc = jnp.zeros((S, LANES - MLA_NOPE - MLA_ROPE), F32)
    cq = jnp.concatenate([ones, cosf, zc], axis=1)
    sq = jnp.concatenate([jnp.zeros((S, MLA_NOPE), F32), sins, zc], axis=1)
    return cq, sq, cq.T, sq.T


def _prep_l1(w_in, w_out):
    qw = GQA_Q_HEADS * HEAD_DIM
    kw = GQA_KV_HEADS * HEAD_DIM
    wq, wk, wv = w_in[:, :qw], w_in[:, qw:qw + kw], w_in[:, qw + kw:]
    d = w_in.shape[0]

    def partner(w, heads):
        wh = w.reshape(d, heads, HEAD_DIM)
        half = ROT_DIM // 2
        return jnp.concatenate([wh[:, :, half:ROT_DIM], wh[:, :, :half],
                                jnp.zeros((d, heads, HEAD_DIM - ROT_DIM), F32)], axis=2).reshape(d, heads * HEAD_DIM)

    def dup(w):
        return jnp.concatenate([w.reshape(d, GQA_KV_HEADS, HEAD_DIM)] * 2, axis=2).reshape(d, GQA_KV_HEADS * LANES)

    wn = jnp.concatenate([dup(wk), dup(partner(wk, GQA_KV_HEADS))], axis=1).astype(BF16)
    wt = jnp.concatenate([wq, partner(wq, GQA_Q_HEADS), wv], axis=1).T.astype(BF16)
    return dict(wn=wn, wt=wt, wo=w_out.astype(BF16))


def _l1_tables(S):
    cosf, sins = _rope_tables(S, ROPE_THETA, ROT_DIM)
    c64 = jnp.concatenate([cosf, jnp.ones((S, HEAD_DIM - ROT_DIM), F32)], axis=1)
    s64 = jnp.concatenate([sins, jnp.zeros((S, HEAD_DIM - ROT_DIM), F32)], axis=1)
    c128, s128 = jnp.concatenate([c64, c64], axis=1), jnp.concatenate([s64, s64], axis=1)
    return c128, s128, c128.T, s128.T


def _prep_ffn(w_up, conv_w, conv_b, w_down, g, b):
    return dict(wup=w_up.astype(BF16), cw=conv_w, cb=conv_b.reshape(1, -1),
                wdn=w_down.astype(BF16), g=g.reshape(1, -1), b=b.reshape(1, -1))


def _encoder(x, p0, na_tiles, ln0, f0, p1, sinks, ln1, f1):
    B, S, D = x.shape
    qat, ka, vat, qt, kc, vt = _l0_proj(x, p0, _l0_tables(S))
    a_out_t = _na_attention(qat, ka, vat, na_tiles)
    b_out_t = _mla_attention(qt, kc, vt)
    x = _out_proj(x.reshape(B * S, D), [a_out_t, b_out_t], [p0['woa'], p0['wob']], *ln0).reshape(B, S, D)
    x = _ffn(x, f0)
    qt1, k1, vt1 = _l1_proj(x, p1, _l1_tables(S))
    attn_t = _win_attention(qt1, k1, vt1, sinks)
    x = _out_proj(x.reshape(B * S, D), [attn_t], [p1['wo']], *ln1).reshape(B, S, D)
    return _ffn(x, f1)


def kernel(x_prompt, x_sample, l0_w_in, l0_rpb, l0_g_q_norm, l0_w_q_up, l0_g_kv_norm, l0_w_kv_up, l0_w_out, l0_ln1_g, l0_ln1_b, l0_ffn_w_up, l0_ffn_conv_w, l0_ffn_conv_b, l0_ffn_w_down, l0_ln2_g, l0_ln2_b, l1_w_in, l1_sinks, l1_w_out, l1_ln1_g, l1_ln1_b, l1_ffn_w_up, l1_ffn_conv_w, l1_ffn_conv_b, l1_ffn_w_down, l1_ln2_g, l1_ln2_b):
    p0 = _prep_l0(l0_w_in, l0_g_q_norm, l0_w_q_up, l0_g_kv_norm, l0_w_kv_up, l0_w_out)
    na_tiles = _na_bias_tiles(l0_rpb)
    ln0 = (l0_ln1_g.reshape(1, -1), l0_ln1_b.reshape(1, -1))
    f0 = _prep_ffn(l0_ffn_w_up, l0_ffn_conv_w, l0_ffn_conv_b, l0_ffn_w_down, l0_ln2_g, l0_ln2_b)
    p1 = _prep_l1(l1_w_in, l1_w_out)
    ln1 = (l1_ln1_g.reshape(1, -1), l1_ln1_b.reshape(1, -1))
    f1 = _prep_ffn(l1_ffn_w_up, l1_ffn_conv_w, l1_ffn_conv_b, l1_ffn_w_down, l1_ln2_g, l1_ln2_b)
    args = (p0, na_tiles, ln0, f0, p1, l1_sinks.astype(F32), ln1, f1)
    return (_encoder(x_prompt, *args), _encoder(x_sample, *args))
```

```python
import functools

import numpy as np
import jax
import jax.numpy as jnp
from jax import lax
from jax.experimental import pallas as pl
from jax.experimental.pallas import tpu as pltpu

F32 = jnp.float32
BF16 = jnp.bfloat16

D_MODEL = 1024
DEPTH = 2
GRID_W = 64
HEAD_DIM = 64
NA_HEADS = 8
NA_KH = 8
NA_KW = 16
MLA_HEADS = 8
MLA_Q_RANK = 384
MLA_KV_RANK = 256
MLA_NOPE = 64
MLA_ROPE = 32
MLA_V = 64
MLA_THETA = 10000.0
GQA_Q_HEADS = 16
GQA_KV_HEADS = 4
WINDOW = 128
ROPE_THETA = 500000.0
ROT_DIM = HEAD_DIM // 4
D_FF = 2816
ALPHA = (2 * DEPTH) ** 0.25
NORM_EPS = 1e-5
NA_WIDTH = NA_HEADS * HEAD_DIM

LANES = 128
BF16_ROWS = 16
NEG = -1e30
VMEM_LIMIT = 56 * 1024 * 1024

PROJ_TM = 512
OUT_TM = PROJ_TM
FFN_TM = 1024
FFN_CK = 256
FFN_HALO = BF16_ROWS
MLA_TQ = 256
MLA_NSUB = 4
MLA_UNROLL = 2
MLA_VROWS = MLA_V + BF16_ROWS
NA_ROWS = 8
NA_WIN = 16
NA_AHEAD = 2
NA_VROWS = HEAD_DIM + BF16_ROWS
WIN_TQ = 256
WIN_AHEAD = 4
WIN_VROWS = HEAD_DIM + BF16_ROWS

_NT = (((1,), (1,)), ((), ()))
_TN = (((0,), (0,)), ((), ()))


def _params(sem):
    return pltpu.CompilerParams(dimension_semantics=sem, vmem_limit_bytes=VMEM_LIMIT)


def _full(shape):
    n = len(shape)
    return pl.BlockSpec(shape, lambda *_: (0,) * n)


def _layer_norm(z, g, b):
    mu = jnp.mean(z, axis=-1, keepdims=True)
    zc = z - mu
    var = jnp.mean(zc * zc, axis=-1, keepdims=True)
    return zc * lax.rsqrt(var + NORM_EPS) * g + b


def _rms_norm(x, g):
    return x * lax.rsqrt(jnp.mean(x * x, axis=-1, keepdims=True) + NORM_EPS) * g


def _l0_proj_kernel(x_ref, wn_ref, wkat_ref, gq_ref, gkv_ref, wqupt_ref, wkn_ref, wvt_ref,
                    cq_ref, sq_ref, cqt_ref, sqt_ref,
                    qat_ref, ka_ref, vat_ref, qt_ref, kc_ref, vt_ref):
    xb = x_ref[0].astype(BF16)
    hn = jnp.dot(xb, wn_ref[...], preferred_element_type=F32)
    tm = xb.shape[0]
    ka_ref[0] = hn[:, 0:NA_WIDTH].astype(BF16)
    hat = lax.dot_general(wkat_ref[...], xb, _NT, preferred_element_type=F32)
    ones_rows = jnp.where(lax.broadcasted_iota(jnp.int32, (BF16_ROWS, tm), 0) == 0, 1.0, 0.0).astype(BF16)
    for p in range(NA_HEADS // 2):
        qat_ref[0, p] = (hat[p * LANES:(p + 1) * LANES] * (HEAD_DIM ** -0.5)).astype(BF16)
    for h in range(NA_HEADS):
        vat_ref[0, h, 0:HEAD_DIM, :] = hat[NA_WIDTH + h * HEAD_DIM:NA_WIDTH + (h + 1) * HEAD_DIM].astype(BF16)
        vat_ref[0, h, HEAD_DIM:NA_VROWS, :] = ones_rows
    o = NA_WIDTH
    qn = _rms_norm(hn[:, o:o + MLA_Q_RANK], gq_ref[...]).astype(BF16)
    o += MLA_Q_RANK
    kvn = _rms_norm(hn[:, o:o + MLA_KV_RANK], gkv_ref[...]).astype(BF16)
    o += MLA_KV_RANK
    kr = hn[:, o:o + LANES] * cq_ref[...] + hn[:, o + LANES:o + 2 * LANES] * sq_ref[...]
    hw = MLA_HEADS * LANES
    qqt = lax.dot_general(wqupt_ref[...], qn, _NT, preferred_element_type=F32)
    cqt = cqt_ref[...]
    sqt = sqt_ref[...]
    kn = jnp.dot(kvn, wkn_ref[...], preferred_element_type=F32)
    vt = lax.dot_general(wvt_ref[...], kvn, _NT, preferred_element_type=F32)
    for h in range(MLA_HEADS):
        a = qqt[h * LANES:(h + 1) * LANES]
        b = qqt[hw + h * LANES:hw + (h + 1) * LANES]
        qt_ref[0, h, 0] = (a * cqt + b * sqt).astype(BF16)
        kc_ref[0, :, h * LANES:(h + 1) * LANES] = (kn[:, h * LANES:(h + 1) * LANES] + kr).astype(BF16)
        vt_ref[0, h, 0, 0:MLA_V, :] = vt[h * MLA_V:(h + 1) * MLA_V].astype(BF16)
        vt_ref[0, h, 0, MLA_V:MLA_VROWS, :] = ones_rows


def _l0_proj(x, w, tabs):
    B, S, D = x.shape
    tm = PROJ_TM
    nt = S // tm
    hw = MLA_HEADS * LANES
    cq, sq, cqt, sqt = tabs
    row = lambda b, i: (b, i, 0)
    chunk = lambda b, i: (b, 0, i, 0, 0)
    out_shape = (
        jax.ShapeDtypeStruct((B, NA_HEADS // 2, LANES, S), BF16),
        jax.ShapeDtypeStruct((B, S, NA_WIDTH), BF16),
        jax.ShapeDtypeStruct((B, NA_HEADS, NA_VROWS, S), BF16),
        jax.ShapeDtypeStruct((B, MLA_HEADS, nt, LANES, tm), BF16),
        jax.ShapeDtypeStruct((B, S, hw), BF16),
        jax.ShapeDtypeStruct((B, MLA_HEADS, nt, MLA_VROWS, tm), BF16),
    )
    return pl.pallas_call(
        _l0_proj_kernel,
        grid=(B, nt),
        in_specs=[
            pl.BlockSpec((1, tm, D), row),
            _full(w['wn'].shape), _full(w['wkat'].shape), _full(w['gq'].shape), _full(w['gkv'].shape),
            _full(w['wqupt'].shape), _full(w['wkn'].shape), _full(w['wvt'].shape),
            pl.BlockSpec((tm, LANES), lambda b, i: (i, 0)),
            pl.BlockSpec((tm, LANES), lambda b, i: (i, 0)),
            pl.BlockSpec((LANES, tm), lambda b, i: (0, i)),
            pl.BlockSpec((LANES, tm), lambda b, i: (0, i)),
        ],
        out_specs=(
            pl.BlockSpec((1, NA_HEADS // 2, LANES, tm), lambda b, i: (b, 0, 0, i)),
            pl.BlockSpec((1, tm, NA_WIDTH), row),
            pl.BlockSpec((1, NA_HEADS, NA_VROWS, tm), lambda b, i: (b, 0, 0, i)),
            pl.BlockSpec((1, MLA_HEADS, 1, LANES, tm), chunk),
            pl.BlockSpec((1, tm, hw), row),
            pl.BlockSpec((1, MLA_HEADS, 1, MLA_VROWS, tm), chunk),
        ),
        out_shape=out_shape,
        compiler_params=_params(("parallel", "parallel")),
        name="l0_proj",
    )(x, w['wn'], w['wkat'], w['gq'], w['gkv'], w['wqupt'], w['wkn'], w['wvt'], cq, sq, cqt, sqt)


def _na_kernel(q_ref, k0_ref, k1_ref, k2_ref, k3_ref, v0_ref, v1_ref, v2_ref, v3_ref, z2_ref, o_ref,
               tab_scr, kw_scr, vw_scr):
    r = pl.program_id(1)
    nblk = pl.num_programs(1)
    case = jnp.where(r == 0, 0, jnp.where(r == nblk - 1, 2, 1))

    @pl.when((r <= 1) | (r == nblk - 1))
    def _():
        left = lax.broadcasted_iota(jnp.int32, (GRID_W, LANES), 1) < GRID_W

        def penalty(i, j):
            first_key = jnp.where(case == 0, max(i - NA_KH // 2, 0),
                                  jnp.where(case == 1, i, min(i + NA_KH // 2, NA_ROWS)))
            return jnp.where((j >= first_key) & (j < first_key + NA_KH), 0.0, NEG)

        for j in range(NA_WIN):
            for a in range(NA_ROWS // 2):
                pen = jnp.where(left, penalty(2 * a, j), penalty(2 * a + 1, j)).astype(F32)
                e = jnp.clip(j - 2 * a + NA_KH - 1 - (NA_KH // 2) * case, 0, 2 * NA_KH - 1)
                for h in range(NA_HEADS):
                    tab_scr[h, j * GRID_W:(j + 1) * GRID_W, a * LANES:(a + 1) * LANES] = z2_ref[h, e] + pen

    tkp = k0_ref.shape[1]
    for t, (k_ref, v_ref) in enumerate(((k0_ref, v0_ref), (k1_ref, v1_ref), (k2_ref, v2_ref), (k3_ref, v3_ref))):
        kw_scr[t * tkp:(t + 1) * tkp, :] = k_ref[0]
        vw_scr[:, :, t * tkp:(t + 1) * tkp] = v_ref[0]
    upper = lax.broadcasted_iota(jnp.int32, q_ref.shape[2:], 0) >= HEAD_DIM

    def scores(h):
        qt = q_ref[0, h // 2]
        qh = jnp.where(upper == (h % 2 == 1), qt, jnp.zeros_like(qt))
        kw = kw_scr[:, (h // 2) * LANES:(h // 2 + 1) * LANES]
        return jnp.dot(kw, qh, preferred_element_type=F32) + tab_scr[h]

    pending = [scores(h) for h in range(NA_AHEAD)]
    for h in range(NA_HEADS):
        st = pending.pop(0)
        pt = jnp.exp(st - jnp.max(st, axis=0, keepdims=True)).astype(BF16)
        if h + NA_AHEAD < NA_HEADS:
            pending.append(scores(h + NA_AHEAD))
        acc = jnp.dot(vw_scr[h], pt, preferred_element_type=F32)
        o_ref[0, h, 0] = (acc[0:HEAD_DIM] / acc[HEAD_DIM:HEAD_DIM + 1]).astype(BF16)


def _na_attention(qat, ka, vat, z2):
    B, S, _ = ka.shape
    rows = S // GRID_W
    nblk = rows // NA_ROWS
    tq = NA_ROWS * GRID_W
    tkp = (NA_WIN // 4) * GRID_W
    npieces = S // tkp
    hp_n = NA_HEADS // 2
    assert nblk >= 2 and rows % NA_ROWS == 0 and tq == OUT_TM

    def wstart(r):
        return jnp.clip(2 * r - 1, 0, npieces - 4)

    k_specs = [pl.BlockSpec((1, tkp, NA_WIDTH), lambda b, r, t=t: (b, wstart(r) + t, 0)) for t in range(4)]
    v_specs = [pl.BlockSpec((1, NA_HEADS, NA_VROWS, tkp), lambda b, r, t=t: (b, 0, 0, wstart(r) + t))
               for t in range(4)]
    return pl.pallas_call(
        _na_kernel,
        grid=(B, nblk),
        in_specs=[pl.BlockSpec((1, hp_n, LANES, tq), lambda b, r: (b, 0, 0, r))] + k_specs + v_specs + [
            pl.BlockSpec(z2.shape, lambda b, r: (0, 0, 0, 0), pipeline_mode=pl.Buffered(1))],
        out_specs=pl.BlockSpec((1, NA_HEADS, 1, HEAD_DIM, tq), lambda b, r: (b, 0, r, 0, 0)),
        out_shape=jax.ShapeDtypeStruct((B, NA_HEADS, nblk, HEAD_DIM, tq), BF16),
        scratch_shapes=[pltpu.VMEM((NA_HEADS, 4 * tkp, tq), F32), pltpu.VMEM((4 * tkp, NA_WIDTH), BF16),
                        pltpu.VMEM((NA_HEADS, NA_VROWS, 4 * tkp), BF16)],
        compiler_params=_params(("arbitrary", "arbitrary")),
        name="na_attn",
    )(qat, ka, ka, ka, ka, vat, vat, vat, vat, z2)


def _na_bias_tiles(rpb):
    cq = np.arange(GRID_W)[None, :]
    kc = np.arange(GRID_W)[:, None]
    cs = np.clip(cq - NA_KW // 2, 0, GRID_W - NA_KW)
    col_ok = (kc >= cs) & (kc < cs + NA_KW)
    onehot = ((kc - cq + NA_KW - 1)[None] == np.arange(2 * NA_KW - 1)[:, None, None]) & col_ok[None]
    y = jnp.einsum('hab,bkq->hakq', rpb.astype(F32), jnp.asarray(onehot, F32), precision=lax.Precision.HIGHEST)
    y = jnp.where(col_ok[None, None], y, NEG)
    y = jnp.pad(y, ((0, 0), (1, 1), (0, 0), (0, 0)), constant_values=NEG)
    return jnp.concatenate([y[:, 1:], y[:, :-1]], axis=-1)


def _mla_kernel(q_ref, k_ref, v_ref, o_ref, s_scr, *, nq, nk, tk, tq, nsub, unroll, c):
    per_chunk = tk // tq

    def q_body(qi, carry):
        where = [(qi * (nsub // per_chunk) + j // per_chunk, (j % per_chunk) * tq) for j in range(nsub)]
        qts = [q_ref[0, 0, ci, :, lo:lo + tq] for ci, lo in where]

        def scores(kc, slot):
            k = k_ref[0, pl.ds(pl.multiple_of(kc * tk, tk), tk), :]
            for j, qt in enumerate(qts):
                s_scr[slot, j] = jnp.dot(k, qt, preferred_element_type=F32)

        def fold(kc, slot, mc):
            v = v_ref[0, 0, kc]
            stats = []
            for j, (m_old, _) in enumerate(mc):
                st = s_scr[slot, j]
                m_new = jnp.maximum(m_old, jnp.max(st, axis=0, keepdims=True))
                alpha = jnp.exp2((m_old - m_new) * c)
                pt = jnp.exp2((st - m_new) * c).astype(BF16)
                stats.append((m_new, alpha, pt))
            return tuple((m_new, acc * alpha + jnp.dot(v, pt, preferred_element_type=F32))
                         for (_, acc), (m_new, alpha, pt) in zip(mc, stats))

        def step(kc, slot, mc):
            k = k_ref[0, pl.ds(pl.multiple_of((kc + 1) * tk, tk), tk), :]
            v = v_ref[0, 0, kc]
            out = []
            for j, (m_old, acc) in enumerate(mc):
                st = s_scr[slot, j]
                m_new = jnp.maximum(m_old, jnp.max(st, axis=0, keepdims=True))
                alpha = jnp.exp2((m_old - m_new) * c)
                pt = jnp.exp2((st - m_new) * c).astype(BF16)
                s_scr[1 - slot, j] = jnp.dot(k, qts[j], preferred_element_type=F32)
                out.append((m_new, acc * alpha + jnp.dot(v, pt, preferred_element_type=F32)))
            return tuple(out)

        def k_body(i, mc):
            return step(2 * i + 1, 1, step(2 * i, 0, mc))

        init = tuple((jnp.full((1, tq), -jnp.inf, F32), jnp.zeros((MLA_VROWS, tq), F32)) for _ in range(nsub))
        scores(0, 0)
        mc = lax.fori_loop(0, nk // 2 - 1, k_body, init, unroll=unroll)
        res = fold(nk - 1, 1, step(nk - 2, 0, mc))
        for (_, acc), (ci, lo) in zip(res, where):
            o_ref[0, 0, ci, :, lo:lo + tq] = (acc[0:MLA_V] / acc[MLA_V:MLA_V + 1]).astype(BF16)
        return carry

    lax.fori_loop(0, nq, q_body, 0)


def _mla_attention(qt, kc, vt):
    B, H, nt, _, tm = qt.shape
    S = nt * tm
    c = float((MLA_NOPE + MLA_ROPE) ** -0.5 * np.log2(np.e))
    head = lambda b, h: (b, h, 0, 0, 0)
    nsub = MLA_NSUB
    assert nsub % (tm // MLA_TQ) == 0 and (nt * (tm // MLA_TQ)) % nsub == 0 and nt % 2 == 0
    return pl.pallas_call(
        functools.partial(_mla_kernel, nq=nt * (tm // MLA_TQ) // nsub, nk=nt, tk=tm, tq=MLA_TQ, nsub=nsub,
                          unroll=MLA_UNROLL * 2 if nt >= 32 else MLA_UNROLL, c=c),
        grid=(B, H),
        in_specs=[
            pl.BlockSpec((1, 1, nt, LANES, tm), head),
            pl.BlockSpec((1, S, LANES), lambda b, h: (b, 0, h)),
            pl.BlockSpec((1, 1, nt, MLA_VROWS, tm), head),
        ],
        out_specs=pl.BlockSpec((1, 1, nt, MLA_V, tm), head),
        out_shape=jax.ShapeDtypeStruct((B, H, nt, MLA_V, tm), BF16),
        scratch_shapes=[pltpu.VMEM((2, nsub, tm, MLA_TQ), F32)],
        compiler_params=_params(("parallel", "parallel")),
        name="mla_attn",
    )(qt, kc, vt)


def _out_proj_kernel(*refs, n_act):
    x_ref = refs[0]
    a_refs = refs[1:1 + n_act]
    w_refs = refs[1 + n_act:1 + 2 * n_act]
    g_ref, b_ref, o_ref = refs[1 + 2 * n_act:]
    mix = None
    for a, w in zip(a_refs, w_refs):
        at = a[0, :, 0]
        at = at.reshape(at.shape[0] * at.shape[1], at.shape[2])
        t = lax.dot_general(at, w[...], _TN, preferred_element_type=F32)
        mix = t if mix is None else mix + t
    o_ref[...] = _layer_norm(ALPHA * x_ref[...] + mix, g_ref[...], b_ref[...])


def _out_proj(x2, acts_t, ws, g, b):
    N, D = x2.shape
    tm = OUT_TM
    row = lambda i: (i, 0)
    specs = []
    for a in acts_t:
        _, H, nt, hd, tmc = a.shape
        assert tmc == tm
        specs.append(pl.BlockSpec((1, H, 1, hd, tm), lambda i, nt=nt: (i // nt, 0, i % nt, 0, 0)))
    return pl.pallas_call(
        functools.partial(_out_proj_kernel, n_act=len(acts_t)),
        grid=(N // tm,),
        in_specs=[pl.BlockSpec((tm, D), row)] + specs + [_full(w.shape) for w in ws] + [_full(g.shape), _full(b.shape)],
        out_specs=pl.BlockSpec((tm, D), row),
        out_shape=jax.ShapeDtypeStruct((N, D), F32),
        compiler_params=_params(("parallel",)),
        name="out_proj",
    )(x2, *acts_t, *ws, g, b)


def _ffn_kernel(x_ref, xp_ref, xn_ref, wup_ref, cw_ref, cb_ref, wdn_ref, g_ref, b_ref, o_ref, xb_scr, act_scr):
    i = pl.program_id(1)
    tm = x_ref.shape[1]
    hl = FFN_HALO
    ck = FFN_CK
    xp = jnp.where(i > 0, xp_ref[0], jnp.zeros_like(xp_ref[0]))
    xn = jnp.where(i < pl.num_programs(1) - 1, xn_ref[0], jnp.zeros_like(xn_ref[0]))
    xb_scr[0:hl, :] = xp.astype(BF16)
    xb_scr[hl:hl + tm, :] = x_ref[0].astype(BF16)
    xb_scr[hl + tm:hl + tm + hl, :] = xn.astype(BF16)

    def conv(c0):
        h = jnp.dot(xb_scr[...], wup_ref[:, c0:c0 + ck], preferred_element_type=F32)
        cw = cw_ref[:, c0:c0 + ck]
        return (h[hl - 1:hl - 1 + tm] * cw[0:1] + h[hl:hl + tm] * cw[1:2] + h[hl + 1:hl + 1 + tm] * cw[2:3]
                + cb_ref[:, c0:c0 + ck])

    for c in range(D_FF // ck):
        gate = conv(c * ck)
        val = conv(D_FF + c * ck)
        act = 0.5 * gate * (1.0 + lax.erf(gate * (2.0 ** -0.5))) * val
        act_scr[:, c * ck:(c + 1) * ck] = act.astype(BF16)
    y = jnp.dot(act_scr[...], wdn_ref[...], preferred_element_type=F32)
    o_ref[0] = _layer_norm(ALPHA * x_ref[0] + y, g_ref[...], b_ref[...])


def _ffn(x, w):
    B, S, D = x.shape
    tm = min(FFN_TM, S)
    hl = FFN_HALO
    nh = S // hl
    r = tm // hl
    once = dict(pipeline_mode=pl.Buffered(1))
    return pl.pallas_call(
        _ffn_kernel,
        grid=(B, S // tm),
        in_specs=[
            pl.BlockSpec((1, tm, D), lambda b, i: (b, i, 0)),
            pl.BlockSpec((1, hl, D), lambda b, i: (b, jnp.maximum(i * r - 1, 0), 0)),
            pl.BlockSpec((1, hl, D), lambda b, i: (b, jnp.minimum((i + 1) * r, nh - 1), 0)),
            pl.BlockSpec(w['wup'].shape, lambda b, i: (0, 0), **once),
            _full(w['cw'].shape), _full(w['cb'].shape),
            pl.BlockSpec(w['wdn'].shape, lambda b, i: (0, 0), **once),
            _full(w['g'].shape), _full(w['b'].shape),
        ],
        out_specs=pl.BlockSpec((1, tm, D), lambda b, i: (b, i, 0)),
        out_shape=jax.ShapeDtypeStruct((B, S, D), F32),
        scratch_shapes=[pltpu.VMEM((tm + 2 * hl, D), BF16), pltpu.VMEM((tm, D_FF), BF16)],
        compiler_params=_params(("parallel", "parallel")),
        name="ffn",
    )(x, x, x, w['wup'], w['cw'], w['cb'], w['wdn'], w['g'], w['b'])


def _l1_proj_kernel(x_ref, wn_ref, wt_ref, ck_ref, sk_ref, cqt_ref, sqt_ref, qt_ref, k_ref, vt_ref):
    xb = x_ref[0].astype(BF16)
    tm = xb.shape[0]
    hn = jnp.dot(xb, wn_ref[...], preferred_element_type=F32)
    ht = lax.dot_general(wt_ref[...], xb, _NT, preferred_element_type=F32)
    qw = GQA_Q_HEADS * HEAD_DIM
    kw2 = GQA_KV_HEADS * LANES
    ck = ck_ref[...]
    sk = sk_ref[...]
    for g in range(GQA_KV_HEADS):
        k_ref[0, :, g * LANES:(g + 1) * LANES] = (
            hn[:, g * LANES:(g + 1) * LANES] * ck + hn[:, kw2 + g * LANES:kw2 + (g + 1) * LANES] * sk).astype(BF16)
    cqt = cqt_ref[...]
    sqt = sqt_ref[...]
    for p in range(qw // LANES):
        a = ht[p * LANES:(p + 1) * LANES]
        b = ht[qw + p * LANES:qw + (p + 1) * LANES]
        qt_ref[0, p] = ((a * cqt + b * sqt) * (HEAD_DIM ** -0.5)).astype(BF16)
    ones_rows = jnp.where(lax.broadcasted_iota(jnp.int32, (BF16_ROWS, tm), 0) == 0, 1.0, 0.0).astype(BF16)
    for g in range(GQA_KV_HEADS):
        vt_ref[0, g, 0:HEAD_DIM, :] = ht[2 * qw + g * HEAD_DIM:2 * qw + (g + 1) * HEAD_DIM].astype(BF16)
        vt_ref[0, g, HEAD_DIM:WIN_VROWS, :] = ones_rows


def _l1_proj(x, w, tabs):
    B, S, D = x.shape
    tm = PROJ_TM
    ck, sk, cqt, sqt = tabs
    npair = GQA_Q_HEADS // 2
    kw2 = GQA_KV_HEADS * LANES
    row = lambda b, i: (b, i, 0)
    return pl.pallas_call(
        _l1_proj_kernel,
        grid=(B, S // tm),
        in_specs=[
            pl.BlockSpec((1, tm, D), row), _full(w['wn'].shape), _full(w['wt'].shape),
            pl.BlockSpec((tm, LANES), lambda b, i: (i, 0)),
            pl.BlockSpec((tm, LANES), lambda b, i: (i, 0)),
            pl.BlockSpec((LANES, tm), lambda b, i: (0, i)),
            pl.BlockSpec((LANES, tm), lambda b, i: (0, i)),
        ],
        out_specs=(
            pl.BlockSpec((1, npair, LANES, tm), lambda b, i: (b, 0, 0, i)),
            pl.BlockSpec((1, tm, kw2), row),
            pl.BlockSpec((1, GQA_KV_HEADS, WIN_VROWS, tm), lambda b, i: (b, 0, 0, i)),
        ),
        out_shape=(
            jax.ShapeDtypeStruct((B, npair, LANES, S), BF16),
            jax.ShapeDtypeStruct((B, S, kw2), BF16),
            jax.ShapeDtypeStruct((B, GQA_KV_HEADS, WIN_VROWS, S), BF16),
        ),
        compiler_params=_params(("parallel", "parallel")),
        name="l1_proj",
    )(x, w['wn'], w['wt'], ck, sk, cqt, sqt)


def _win_kernel(sink_ref, q_ref, kp_ref, km_ref, kn_ref, vp_ref, vm_ref, vn_ref, o_ref, kw_scr, vw_scr):
    i = pl.program_id(1)
    last = pl.num_programs(1) - 1
    tq = q_ref.shape[3]
    hb = kp_ref.shape[1]
    tk = tq + 2 * hb
    kw_scr[0:hb, :] = kp_ref[0]
    kw_scr[hb:hb + tq, :] = km_ref[0]
    kw_scr[hb + tq:tk, :] = kn_ref[0]
    vw_scr[:, :, 0:hb] = vp_ref[0]
    vw_scr[:, :, hb:hb + tq] = vm_ref[0]
    vw_scr[:, :, hb + tq:tk] = vn_ref[0]
    kr = lax.broadcasted_iota(jnp.int32, (tk, tq), 0)
    qc = lax.broadcasted_iota(jnp.int32, (tk, tq), 1)
    ok = (jnp.abs(kr - hb - qc) <= WINDOW) & ((kr >= hb) | (i > 0)) & ((kr < hb + tq) | (i < last))
    mask = jnp.where(ok, 0.0, NEG).astype(F32)
    upper = lax.broadcasted_iota(jnp.int32, (LANES, tq), 0) >= HEAD_DIM
    group = GQA_Q_HEADS // GQA_KV_HEADS

    def scores(h):
        g = h // group
        q2 = q_ref[0, h // 2]
        qh = jnp.where(upper == (h % 2 == 1), q2, jnp.zeros_like(q2))
        return jnp.dot(kw_scr[:, g * LANES:(g + 1) * LANES], qh, preferred_element_type=F32) + mask

    pending = [scores(h) for h in range(WIN_AHEAD)]
    for h in range(GQA_Q_HEADS):
        st = pending.pop(0)
        sink = sink_ref[h]
        m = jnp.maximum(jnp.max(st, axis=0, keepdims=True), sink)
        pt = jnp.exp(st - m).astype(BF16)
        esink = jnp.exp(sink - m)
        if h + WIN_AHEAD < GQA_Q_HEADS:
            pending.append(scores(h + WIN_AHEAD))
        acc = jnp.dot(vw_scr[h // group], pt, preferred_element_type=F32)
        o_ref[0, h, 0] = (acc[0:HEAD_DIM] / (acc[HEAD_DIM:HEAD_DIM + 1] + esink)).astype(BF16)


def _win_attention(qt, k, vt, sinks):
    B, npair, _, S = qt.shape
    tq = min(WIN_TQ, S)
    hb = WINDOW
    r = tq // hb
    nhb = S // hb
    kw2 = k.shape[2]
    tm = OUT_TM
    per = tm // tq
    prev = lambda i: jnp.maximum(i * r - 1, 0)
    nxt = lambda i: jnp.minimum((i + 1) * r, nhb - 1)
    return pl.pallas_call(
        _win_kernel,
        grid_spec=pltpu.PrefetchScalarGridSpec(
            num_scalar_prefetch=1,
            grid=(B, S // tq),
            in_specs=[
                pl.BlockSpec((1, npair, LANES, tq), lambda b, i, s: (b, 0, 0, i)),
                pl.BlockSpec((1, hb, kw2), lambda b, i, s: (b, prev(i), 0)),
                pl.BlockSpec((1, tq, kw2), lambda b, i, s: (b, i, 0)),
                pl.BlockSpec((1, hb, kw2), lambda b, i, s: (b, nxt(i), 0)),
                pl.BlockSpec((1, GQA_KV_HEADS, WIN_VROWS, hb), lambda b, i, s: (b, 0, 0, prev(i))),
                pl.BlockSpec((1, GQA_KV_HEADS, WIN_VROWS, tq), lambda b, i, s: (b, 0, 0, i)),
                pl.BlockSpec((1, GQA_KV_HEADS, WIN_VROWS, hb), lambda b, i, s: (b, 0, 0, nxt(i))),
            ],
            out_specs=pl.BlockSpec((1, GQA_Q_HEADS, 1, HEAD_DIM, tq), lambda b, i, s: (b, 0, i // per, 0, i % per)),
            scratch_shapes=[pltpu.VMEM((tq + 2 * hb, kw2), BF16),
                            pltpu.VMEM((GQA_KV_HEADS, WIN_VROWS, tq + 2 * hb), BF16)],
        ),
        out_shape=jax.ShapeDtypeStruct((B, GQA_Q_HEADS, S // tm, HEAD_DIM, tm), BF16),
        compiler_params=_params(("parallel", "parallel")),
        name="win_attn",
    )(sinks, qt, k, k, k, vt, vt, vt)


def _rope_tables(S, theta, rot_dim):
    inv = 1.0 / (theta ** (jnp.arange(0, rot_dim, 2, dtype=F32) / rot_dim))
    ang = jnp.arange(S, dtype=F32)[:, None] * inv[None, :]
    cos, sin = jnp.cos(ang), jnp.sin(ang)
    return jnp.concatenate([cos, cos], axis=1), jnp.concatenate([-sin, sin], axis=1)


def _swap_halves_cols(w, rot_dim):
    half = rot_dim // 2
    return jnp.concatenate([w[:, half:rot_dim], w[:, :half]], axis=1)


def _prep_l0(w_in, g_q, w_q_up, g_kv, w_kv_up, w_out):
    d = w_in.shape[0]
    s = np.cumsum([0, NA_WIDTH, NA_WIDTH, NA_WIDTH, MLA_Q_RANK, MLA_KV_RANK, MLA_ROPE])
    w_qa, w_ka, w_va, w_ql, w_kvl, w_kr = [w_in[:, s[k]:s[k + 1]] for k in range(6)]
    H = MLA_HEADS
    qd = MLA_NOPE + MLA_ROPE

    def rope_block(w):
        return jnp.concatenate([jnp.zeros((d, MLA_NOPE), F32), w, jnp.zeros((d, LANES - qd), F32)], axis=1)

    wn = jnp.concatenate([w_ka, w_ql, w_kvl, rope_block(w_kr), rope_block(_swap_halves_cols(w_kr, MLA_ROPE))],
                         axis=1).astype(BF16)
    wq = w_q_up.reshape(MLA_Q_RANK, H, qd)
    pad = jnp.zeros((MLA_Q_RANK, H, LANES - qd), F32)
    wq_main = jnp.concatenate([wq, pad], axis=2)
    wq_sw = jnp.concatenate([jnp.zeros((MLA_Q_RANK, H, MLA_NOPE), F32),
                             wq[:, :, MLA_NOPE + MLA_ROPE // 2:], wq[:, :, MLA_NOPE:MLA_NOPE + MLA_ROPE // 2],
                             pad], axis=2)
    wqupt = jnp.concatenate([wq_main.reshape(MLA_Q_RANK, H * LANES), wq_sw.reshape(MLA_Q_RANK, H * LANES)],
                            axis=1).T.astype(BF16)
    wkv = w_kv_up.reshape(MLA_KV_RANK, H, MLA_NOPE + MLA_V)
    wkn = jnp.concatenate([wkv[:, :, :MLA_NOPE], jnp.zeros((MLA_KV_RANK, H, LANES - MLA_NOPE), F32)],
                          axis=2).reshape(MLA_KV_RANK, H * LANES).astype(BF16)
    wvt = wkv[:, :, MLA_NOPE:].reshape(MLA_KV_RANK, H * MLA_V).T.astype(BF16)
    wkat = jnp.concatenate([w_qa, w_va], axis=1).T.astype(BF16)
    return dict(wn=wn, wkat=wkat, gq=g_q.reshape(1, -1), gkv=g_kv.reshape(1, -1), wqupt=wqupt,
                wkn=wkn, wvt=wvt, woa=w_out[:NA_WIDTH].astype(BF16), wob=w_out[NA_WIDTH:].astype(BF16))


def _l0_tables(S):
    cosf, sins = _rope_tables(S, MLA_THETA, MLA_ROPE)
    ones = jnp.ones((S, MLA_NOPE), F32)
    zc = jnp.zeros((S, LANES - MLA_NOPE - MLA_ROPE), F32)
    cq = jnp.concatenate([ones, cosf, zc], axis=1)
    sq = jnp.concatenate([jnp.zeros((S, MLA_NOPE), F32), sins, zc], axis=1)
    return cq, sq, cq.T, sq.T


def _prep_l1(w_in, w_out):
    qw = GQA_Q_HEADS * HEAD_DIM
    kw = GQA_KV_HEADS * HEAD_DIM
    wq, wk, wv = w_in[:, :qw], w_in[:, qw:qw + kw], w_in[:, qw + kw:]
    d = w_in.shape[0]

    def partner(w, heads):
        wh = w.reshape(d, heads, HEAD_DIM)
        half = ROT_DIM // 2
        return jnp.concatenate([wh[:, :, half:ROT_DIM], wh[:, :, :half],
                                jnp.zeros((d, heads, HEAD_DIM - ROT_DIM), F32)], axis=2).reshape(d, heads * HEAD_DIM)

    def dup(w):
        return jnp.concatenate([w.reshape(d, GQA_KV_HEADS, HEAD_DIM)] * 2, axis=2).reshape(d, GQA_KV_HEADS * LANES)

    wn = jnp.concatenate([dup(wk), dup(partner(wk, GQA_KV_HEADS))], axis=1).astype(BF16)
    wt = jnp.concatenate([wq, partner(wq, GQA_Q_HEADS), wv], axis=1).T.astype(BF16)
    return dict(wn=wn, wt=wt, wo=w_out.astype(BF16))


def _l1_tables(S):
    cosf, sins = _rope_tables(S, ROPE_THETA, ROT_DIM)
    c64 = jnp.concatenate([cosf, jnp.ones((S, HEAD_DIM - ROT_DIM), F32)], axis=1)
    s64 = jnp.concatenate([sins, jnp.zeros((S, HEAD_DIM - ROT_DIM), F32)], axis=1)
    c128, s128 = jnp.concatenate([c64, c64], axis=1), jnp.concatenate([s64, s64], axis=1)
    return c128, s128, c128.T, s128.T


def _prep_ffn(w_up, conv_w, conv_b, w_down, g, b):
    return dict(wup=w_up.astype(BF16), cw=conv_w, cb=conv_b.reshape(1, -1),
                wdn=w_down.astype(BF16), g=g.reshape(1, -1), b=b.reshape(1, -1))


def _encoder(x, p0, na_tiles, ln0, f0, p1, sinks, ln1, f1):
    B, S, D = x.shape
    qat, ka, vat, qt, kc, vt = _l0_proj(x, p0, _l0_tables(S))
    a_out_t = _na_attention(qat, ka, vat, na_tiles)
    b_out_t = _mla_attention(qt, kc, vt)
    x = _out_proj(x.reshape(B * S, D), [a_out_t, b_out_t], [p0['woa'], p0['wob']], *ln0).reshape(B, S, D)
    x = _ffn(x, f0)
    qt1, k1, vt1 = _l1_proj(x, p1, _l1_tables(S))
    attn_t = _win_attention(qt1, k1, vt1, sinks)
    x = _out_proj(x.reshape(B * S, D), [attn_t], [p1['wo']], *ln1).reshape(B, S, D)
    return _ffn(x, f1)


def kernel(x_prompt, x_sample, l0_w_in, l0_rpb, l0_g_q_norm, l0_w_q_up, l0_g_kv_norm, l0_w_kv_up, l0_w_out, l0_ln1_g, l0_ln1_b, l0_ffn_w_up, l0_ffn_conv_w, l0_ffn_conv_b, l0_ffn_w_down, l0_ln2_g, l0_ln2_b, l1_w_in, l1_sinks, l1_w_out, l1_ln1_g, l1_ln1_b, l1_ffn_w_up, l1_ffn_conv_w, l1_ffn_conv_b, l1_ffn_w_down, l1_ln2_g, l1_ln2_b):
    p0 = _prep_l0(l0_w_in, l0_g_q_norm, l0_w_q_up, l0_g_kv_norm, l0_w_kv_up, l0_w_out)
    na_tiles = _na_bias_tiles(l0_rpb)
    ln0 = (l0_ln1_g.reshape(1, -1), l0_ln1_b.reshape(1, -1))
    f0 = _prep_ffn(l0_ffn_w_up, l0_ffn_conv_w, l0_ffn_conv_b, l0_ffn_w_down, l0_ln2_g, l0_ln2_b)
    p1 = _prep_l1(l1_w_in, l1_w_out)
    ln1 = (l1_ln1_g.reshape(1, -1), l1_ln1_b.reshape(1, -1))
    f1 = _prep_ffn(l1_ffn_w_up, l1_ffn_conv_w, l1_ffn_conv_b, l1_ffn_w_down, l1_ln2_g, l1_ln2_b)
    args = (p0, na_tiles, ln0, f0, p1, l1_sinks.astype(F32), ln1, f1)
    return (_encoder(x_prompt, *args), _encoder(x_sample, *args))
```
